```python
import jax
import jax.numpy as jnp
from jax import lax
import numpy as np

D_MODEL = 1024
BATCH = 16
SEQ = 4096
DEPTH = 1

HEAD_DIM = 64
ATTN_GROUPS = ((128, 1), (512, 4), (2048, 16))
HEADS_PER_GROUP = 4
N_ATTN_HEADS = HEADS_PER_GROUP * len(ATTN_GROUPS)
ATTN_WIDTH = N_ATTN_HEADS * HEAD_DIM
ATTN_OUT_WIDTH = HEADS_PER_GROUP * HEAD_DIM
LRU_WIDTH = D_MODEL
LRU_BLOCKS = 16
LRU_BLOCK_WIDTH = LRU_WIDTH // LRU_BLOCKS
LRU_C = 8.0
CONV_WIDTH = 4
N_EXPERTS = 16
CAPACITY_FACTOR = 2
EXPERT_FF = 2 * D_MODEL
ROPE_THETA = 10000.0
NORM_EPS = 1e-6
NEG_INF = -1e30
IN_WIDTH = 3 * ATTN_WIDTH + 2 * LRU_WIDTH + 2 * D_MODEL
IN_SPLITS = (ATTN_WIDTH, 2 * ATTN_WIDTH, 3 * ATTN_WIDTH, 3 * ATTN_WIDTH + LRU_WIDTH,
             3 * ATTN_WIDTH + 2 * LRU_WIDTH, 3 * ATTN_WIDTH + 2 * LRU_WIDTH + D_MODEL)

kernel_name = 'hybrid_dilated_attn_rglru_ec_moe_block'


def rms_norm(t, w):
    t32 = t.astype(jnp.float32)
    y = t32 * lax.rsqrt(jnp.mean(t32 * t32, axis=-1, keepdims=True) + NORM_EPS)
    return (y * w.astype(jnp.float32)).astype(t.dtype)


def modulate(t, shift, scale):
    return t * (1.0 + scale[:, None, :]) + shift[:, None, :]


def apply_rope(t, positions):
    half = HEAD_DIM // 2
    inv_freq = ROPE_THETA ** (-jnp.arange(half, dtype=jnp.float32) / half)
    ang = positions.astype(jnp.float32)[:, :, None, None] * inv_freq
    cos, sin = jnp.cos(ang), jnp.sin(ang)
    t32 = t.astype(jnp.float32)
    t1, t2 = t32[..., :half], t32[..., half:]
    return jnp.concatenate([t1 * cos - t2 * sin, t2 * cos + t1 * sin], axis=-1).astype(t.dtype)


def banded_attention(q, k, v, half):
    n, length, heads, dh = q.shape
    blk = half
    nb = -(-length // blk)
    extra = nb * blk - length
    q_b = jnp.pad(q, ((0, 0), (0, extra), (0, 0), (0, 0))).reshape(n, nb, blk, heads, dh)
    kv_pad = ((0, 0), (blk, blk + extra), (0, 0), (0, 0))
    k_b = jnp.pad(k, kv_pad).reshape(n, nb + 2, blk, heads, dh)
    v_b = jnp.pad(v, kv_pad).reshape(n, nb + 2, blk, heads, dh)

    def band(t):
        return jnp.concatenate([t[:, :-2], t[:, 1:-1], t[:, 2:]], axis=2)

    k_w, v_w = band(k_b), band(v_b)
    valid = jnp.pad(jnp.ones((length,), dtype=bool), (blk, blk + extra)).reshape(nb + 2, blk)
    valid_w = jnp.concatenate([valid[:-2], valid[1:-1], valid[2:]], axis=1)
    rel = jnp.arange(3 * blk)[None, :] - blk - jnp.arange(blk)[:, None]
    mask = (jnp.abs(rel) <= half)[None] & valid_w[:, None, :]
    s = jnp.einsum('nbqhd,nbkhd->nbhqk', q_b, k_w).astype(jnp.float32) * (dh ** -0.5)
    s = jnp.where(mask[None, :, None], s, NEG_INF)
    m = jnp.max(s, axis=-1, keepdims=True)
    p = jnp.exp(s - m)
    den = jnp.sum(p, axis=-1)
    o = jnp.einsum('nbhqk,nbkhd->nbqhd', p, v_w.astype(jnp.float32))
    o = o / jnp.swapaxes(den, 2, 3)[..., None]
    lse = jnp.swapaxes(m[..., 0] + jnp.log(den), 2, 3)
    o = o.reshape(n, nb * blk, heads, dh)[:, :length]
    lse = lse.reshape(n, nb * blk, heads)[:, :length]
    return o, lse


def dilated_window_attention(q, k, v, window, dilation):
    b, s, heads, dh = q.shape
    length = s // dilation

    def to_classes(t):
        return t.reshape(b, length, dilation, heads, dh).transpose(0, 2, 1, 3, 4).reshape(b * dilation, length, heads, dh)

    o, lse = banded_attention(to_classes(q), to_classes(k), to_classes(v), window // (2 * dilation))
    o = o.reshape(b, dilation, length, heads, dh).transpose(0, 2, 1, 3, 4).reshape(b, s, heads, dh)
    lse = lse.reshape(b, dilation, length, heads).transpose(0, 2, 1, 3).reshape(b, s, heads)
    return o, lse


def block_diag_linear(t, w):
    b, s, ch = t.shape
    y = jnp.einsum('bsnk,nkj->bsnj', t.reshape(b, s, LRU_BLOCKS, LRU_BLOCK_WIDTH), w.astype(t.dtype))
    return y.reshape(b, s, ch)


def rglru_coeffs(xc, w_a, b_a, w_x, b_x, lam):
    r = jax.nn.sigmoid(block_diag_linear(xc, w_a) + b_a.astype(jnp.float32))
    i = jax.nn.sigmoid(block_diag_linear(xc, w_x) + b_x.astype(jnp.float32))
    log_a = -LRU_C * r * jax.nn.softplus(-lam.astype(jnp.float32))
    a = jnp.exp(log_a)
    u = jnp.sqrt(-jnp.expm1(2.0 * log_a)) * (i * xc)
    return a, u


def linear_combine(e1, e2):
    a1, u1 = e1
    a2, u2 = e2
    return a1 * a2, a2 * u1 + u2


def mixer(h, positions, w_in, q_norm_w, k_norm_w, conv_w, conv_b, lru_w_a, lru_b_a, lru_w_x, lru_b_x,
          lru_lambda, w_attn_branch, w_lru_branch, w_out):
    b, s, _ = h.shape
    proj = h @ w_in
    q, k, v, xr, yr, g_attn, g_lru = jnp.split(proj, IN_SPLITS, axis=-1)

    q = apply_rope(rms_norm(q.reshape(b, s, N_ATTN_HEADS, HEAD_DIM), q_norm_w), positions)
    k = apply_rope(rms_norm(k.reshape(b, s, N_ATTN_HEADS, HEAD_DIM), k_norm_w), positions)
    v = v.reshape(b, s, N_ATTN_HEADS, HEAD_DIM)
    outs, lses = [], []
    for g, (window, dilation) in enumerate(ATTN_GROUPS):
        hs = slice(g * HEADS_PER_GROUP, (g + 1) * HEADS_PER_GROUP)
        o, lse = dilated_window_attention(q[:, :, hs], k[:, :, hs], v[:, :, hs], window, dilation)
        outs.append(o)
        lses.append(lse)
    grp_w = jax.nn.softmax(jnp.stack(lses), axis=0)
    attn = jnp.sum(grp_w[..., None] * jnp.stack(outs), axis=0).reshape(b, s, ATTN_OUT_WIDTH).astype(h.dtype)

    pad_l = CONV_WIDTH // 2
    pad_r = CONV_WIDTH - 1 - pad_l
    xc = lax.conv_general_dilated(xr, conv_w.astype(xr.dtype), window_strides=(1,), padding=[(pad_l, pad_r)],
                                  dimension_numbers=('NWC', 'WIO', 'NWC'), feature_group_count=LRU_WIDTH)
    xc = (xc + conv_b).astype(jnp.float32)
    a_f, u_f = rglru_coeffs(xc, lru_w_a[0], lru_b_a[0], lru_w_x[0], lru_b_x[0], lru_lambda[0])
    a_b, u_b = rglru_coeffs(xc, lru_w_a[1], lru_b_a[1], lru_w_x[1], lru_b_x[1], lru_lambda[1])
    _, h_fwd = lax.associative_scan(linear_combine, (a_f, u_f), axis=1)
    _, h_bwd = lax.associative_scan(linear_combine, (a_b, u_b), reverse=True, axis=1)
    lru = ((h_fwd + h_bwd) * jax.nn.gelu(yr.astype(jnp.float32))).astype(h.dtype)

    merged = jax.nn.sigmoid(g_attn) * (attn @ w_attn_branch) + jax.nn.sigmoid(g_lru) * (lru @ w_lru_branch)
    return merged @ w_out


def expert_choice_ffn(h, w_router, w_gate, w_up, w_down):
    b, s, _ = h.shape
    capacity = max(1, CAPACITY_FACTOR * s // N_EXPERTS)
    logits = jnp.einsum('bsd,de->bse', h, w_router).astype(jnp.float32)
    affinity = jax.nn.softmax(logits, axis=-1)
    gates, idx = lax.top_k(jnp.swapaxes(affinity, 1, 2), capacity)
    idx_e = jnp.swapaxes(idx, 0, 1)
    gates_e = jnp.swapaxes(gates, 0, 1).astype(h.dtype)
    bidx = jnp.arange(b)[:, None]

    def run_expert(args):
        ix, gt, wg, wu, wd = args
        xe = h[bidx, ix]
        he = jax.nn.silu(xe @ wg) * (xe @ wu)
        return (he @ wd) * gt[..., None]

    ys = lax.map(run_expert, (idx_e, gates_e, w_gate, w_up, w_down))
    return jnp.zeros_like(h).at[jnp.arange(b)[None, :, None], idx_e].add(ys)


def setup_inputs(seed: int = 0) -> dict:
    key = jax.random.key(seed)
    ks = jax.random.split(key, 26)

    def nrm(k, shape, scale):
        return jax.random.normal(k, shape, jnp.float32) * scale

    L = DEPTH
    a0 = jax.random.uniform(ks[14], (L, 2, LRU_WIDTH), jnp.float32, 0.9, 0.999)
    s0 = a0 ** (1.0 / LRU_C)
    lru_lambda = jnp.log(s0) - jnp.log1p(-s0)
    positions = (jnp.arange(SEQ, dtype=jnp.int32)[None, :]
                 + jax.random.randint(ks[2], (BATCH, 1), 0, 1024, dtype=jnp.int32))
    return {
        'x': nrm(ks[0], (BATCH, SEQ, D_MODEL), 1.0),
        'c': nrm(ks[1], (BATCH, D_MODEL), 1.0),
        'positions': positions,
        'w_ada': nrm(ks[3], (L, D_MODEL, 6 * D_MODEL), D_MODEL ** -0.5),
        'b_ada': nrm(ks[4], (L, 6 * D_MODEL), 0.01),
        'norm1_w': 1.0 + nrm(ks[5], (L, D_MODEL), 0.01),
        'w_in': nrm(ks[6], (L, D_MODEL, IN_WIDTH), D_MODEL ** -0.5),
        'q_norm_w': 1.0 + nrm(ks[7], (L, N_ATTN_HEADS, HEAD_DIM), 0.01),
        'k_norm_w': 1.0 + nrm(ks[8], (L, N_ATTN_HEADS, HEAD_DIM), 0.01),
        'conv_w': nrm(ks[9], (L, CONV_WIDTH, 1, LRU_WIDTH), CONV_WIDTH ** -0.5),
        'conv_b': nrm(ks[10], (L, LRU_WIDTH), 0.01),
        'lru_w_a': nrm(ks[11], (L, 2, LRU_BLOCKS, LRU_BLOCK_WIDTH, LRU_BLOCK_WIDTH), LRU_BLOCK_WIDTH ** -0.5),
        'lru_b_a': nrm(ks[12], (L, 2, LRU_WIDTH), 0.01),
        'lru_w_x': nrm(ks[13], (L, 2, LRU_BLOCKS, LRU_BLOCK_WIDTH, LRU_BLOCK_WIDTH), LRU_BLOCK_WIDTH ** -0.5),
        'lru_b_x': nrm(ks[15], (L, 2, LRU_WIDTH), 0.01),
        'lru_lambda': lru_lambda,
        'w_attn_branch': nrm(ks[16], (L, ATTN_OUT_WIDTH, D_MODEL), ATTN_OUT_WIDTH ** -0.5),
        'w_lru_branch': nrm(ks[17], (L, LRU_WIDTH, D_MODEL), LRU_WIDTH ** -0.5),
        'w_out': nrm(ks[18], (L, D_MODEL, D_MODEL), D_MODEL ** -0.5),
        'norm2_w': 1.0 + nrm(ks[19], (L, D_MODEL), 0.01),
        'w_router': nrm(ks[20], (L, D_MODEL, N_EXPERTS), D_MODEL ** -0.5),
        'w_gate': nrm(ks[21], (L, N_EXPERTS, D_MODEL, EXPERT_FF), D_MODEL ** -0.5),
        'w_up': nrm(ks[22], (L, N_EXPERTS, D_MODEL, EXPERT_FF), D_MODEL ** -0.5),
        'w_down': nrm(ks[23], (L, N_EXPERTS, EXPERT_FF, D_MODEL), EXPERT_FF ** -0.5),
    }


def reference(x, c, positions, w_ada, b_ada, norm1_w, w_in, q_norm_w, k_norm_w, conv_w, conv_b,
              lru_w_a, lru_b_a, lru_w_x, lru_b_x, lru_lambda, w_attn_branch, w_lru_branch, w_out,
              norm2_w, w_router, w_gate, w_up, w_down):
    c_act = jax.nn.silu(c)
    for l in range(DEPTH):
        mod = c_act @ w_ada[l] + b_ada[l]
        shift1, scale1, gate1, shift2, scale2, gate2 = jnp.split(mod, 6, axis=-1)
        h = modulate(rms_norm(x, norm1_w[l]), shift1, scale1)
        mix = mixer(h, positions, w_in[l], q_norm_w[l], k_norm_w[l], conv_w[l], conv_b[l], lru_w_a[l],
                    lru_b_a[l], lru_w_x[l], lru_b_x[l], lru_lambda[l], w_attn_branch[l], w_lru_branch[l], w_out[l])
        x = x + gate1[:, None, :] * mix
        h2 = modulate(rms_norm(x, norm2_w[l]), shift2, scale2)
        ffn = expert_choice_ffn(h2, w_router[l], w_gate[l], w_up[l], w_down[l])
        x = x + gate2[:, None, :] * ffn
    return x
```

```python
import functools

import jax
import jax.numpy as jnp
from jax import lax
from jax.experimental import pallas as pl
from jax.experimental.pallas import tpu as pltpu

F32 = jnp.float32
BF16 = jnp.bfloat16

HEAD_DIM = 64
HEADS_PER_GROUP = 4
GROUP_WIDTH = HEADS_PER_GROUP * HEAD_DIM
ATTN_GROUPS = ((128, 1), (512, 4), (2048, 16))
ATTN_WIDTH = GROUP_WIDTH * len(ATTN_GROUPS)
LRU_BLOCKS = 16
LRU_C = 8.0
CONV_WIDTH = 4
N_EXPERTS = 16
CAPACITY_FACTOR = 2
ROPE_THETA = 10000.0
NORM_EPS = 1e-6
NEG_INF = -1e30

LANES = 128
SUBLANES = 8
MXU_DIM = 256
VMEM_LIMIT = 56 * 1024 * 1024

ROW_TILE = 512
ATTN_TQ = 128
LRU_TC = 128
LRU_NB = 8
LRU_PITCH = LRU_TC + 8
HALO = 16


def _sigmoid(t):
    return 1.0 / (1.0 + jnp.exp(-t))


def _params(*sem):
    return pltpu.CompilerParams(dimension_semantics=sem, vmem_limit_bytes=VMEM_LIMIT)


def _ada_kernel(c_ref, w_ref, b_ref, o_ref):
    c = c_ref[...]
    o_ref[...] = jnp.dot(c * _sigmoid(c), w_ref[...], preferred_element_type=F32,
                         precision=lax.Precision.HIGHEST) + b_ref[...]


def _ada(c, w, b):
    bsz, d = c.shape
    n = w.shape[1]
    tn = n // 4
    return pl.pallas_call(
        _ada_kernel,
        grid=(n // tn,),
        in_specs=[pl.BlockSpec((bsz, d), lambda j: (0, 0)),
                  pl.BlockSpec((d, tn), lambda j: (0, j)),
                  pl.BlockSpec((1, tn), lambda j: (0, j))],
        out_specs=pl.BlockSpec((bsz, tn), lambda j: (0, j)),
        out_shape=jax.ShapeDtypeStruct((bsz, n), F32),
        compiler_params=_params("arbitrary"),
        name="ada",
    )(c, w, b.reshape(1, n))


def _inproj_kernel(x_ref, pos_ref, shift_ref, scale_ref, nw_ref, w_ref, qnw_ref, knw_ref, invf_ref, seg_ref,
                   q_ref, k_ref, v_ref, xr_ref, yr_ref, ga_ref, gl_ref):
    x = x_ref[0]
    d = x.shape[1]
    ms = jnp.mean(x * x, axis=-1, keepdims=True)
    h = x * lax.rsqrt(ms + NORM_EPS) * nw_ref[...]
    hb = (h * (1.0 + scale_ref[0]) + shift_ref[0]).astype(BF16)

    ang = pos_ref[0].astype(F32) * invf_ref[...]
    cos = jnp.cos(ang)
    sin = jnp.sin(ang)
    lane = lax.broadcasted_iota(jnp.int32, (x.shape[0], GROUP_WIDTH), 1)
    first_half = (lane & (HEAD_DIM // 2)) == 0
    cos2 = jnp.concatenate([cos, cos], axis=1)
    sin2 = jnp.concatenate([sin, sin], axis=1)
    sin2 = jnp.where(first_half, -sin2, sin2)
    seg = seg_ref[...]

    def norm_rope(off, w_norm_ref, out_ref, post):
        for j in range(ATTN_WIDTH // GROUP_WIDTH):
            lo_c = j * GROUP_WIDTH
            t = jnp.dot(hb, w_ref[:, off + lo_c:off + lo_c + GROUP_WIDTH], preferred_element_type=F32)
            t2 = t * t
            hi = t2.astype(BF16)
            lo = (t2 - hi.astype(F32)).astype(BF16)
            msq = (jnp.dot(hi, seg, preferred_element_type=F32)
                   + jnp.dot(lo, seg, preferred_element_type=F32)) * (1.0 / HEAD_DIM)
            y = t * lax.rsqrt(msq + NORM_EPS) * w_norm_ref[:, lo_c:lo_c + GROUP_WIDTH]
            rot = jnp.where(first_half, pltpu.roll(y, GROUP_WIDTH - HEAD_DIM // 2, 1),
                            pltpu.roll(y, HEAD_DIM // 2, 1))
            r = y * cos2 + rot * sin2
            out_ref[0, :, lo_c:lo_c + GROUP_WIDTH] = (r * post).astype(out_ref.dtype)

    norm_rope(0, qnw_ref, q_ref, HEAD_DIM ** -0.5)
    norm_rope(ATTN_WIDTH, knw_ref, k_ref, 1.0)
    off = 2 * ATTN_WIDTH
    v_ref[0] = jnp.dot(hb, w_ref[:, off:off + ATTN_WIDTH], preferred_element_type=F32).astype(v_ref.dtype)
    off += ATTN_WIDTH
    for out_ref in (xr_ref, yr_ref, ga_ref, gl_ref):
        out_ref[0] = jnp.dot(hb, w_ref[:, off:off + d], preferred_element_type=F32).astype(out_ref.dtype)
        off += d


def _inproj(x, positions, shift, scale, norm_w, w_in, q_norm_w, k_norm_w):
    bsz, s, d = x.shape
    tm = min(ROW_TILE, s)
    half = HEAD_DIM // 2
    inv_freq = ROPE_THETA ** (-jnp.arange(half, dtype=F32) / half)
    inv_freq = jnp.tile(inv_freq, LANES // half).reshape(1, LANES)
    head_of = jnp.arange(GROUP_WIDTH) // HEAD_DIM
    seg = (head_of[:, None] == head_of[None, :]).astype(BF16)
    row = lambda b, i: (b, i, 0)
    per_batch = lambda b, i: (b, 0, 0)
    const = lambda b, i: (0, 0)
    wide = [jax.ShapeDtypeStruct((bsz, s, d), BF16)] * 4
    narrow = [jax.ShapeDtypeStruct((bsz, s, ATTN_WIDTH), BF16)] * 3
    return pl.pallas_call(
        _inproj_kernel,
        grid=(bsz, s // tm),
        in_specs=[pl.BlockSpec((1, tm, d), row),
                  pl.BlockSpec((1, tm, 1), row),
                  pl.BlockSpec((1, 1, d), per_batch),
                  pl.BlockSpec((1, 1, d), per_batch),
                  pl.BlockSpec((1, d), const),
                  pl.BlockSpec(w_in.shape, const),
                  pl.BlockSpec((1, ATTN_WIDTH), const),
                  pl.BlockSpec((1, ATTN_WIDTH), const),
                  pl.BlockSpec((1, LANES), const),
                  pl.BlockSpec((GROUP_WIDTH, GROUP_WIDTH), const)],
        out_specs=[pl.BlockSpec((1, tm, ATTN_WIDTH), row)] * 3 + [pl.BlockSpec((1, tm, d), row)] * 4,
        out_shape=narrow + wide,
        compiler_params=_params("arbitrary", "arbitrary"),
        name="inproj",
    )(x, positions.reshape(bsz, s, 1), shift.reshape(bsz, 1, d), scale.reshape(bsz, 1, d),
      norm_w.reshape(1, d), w_in.astype(BF16), q_norm_w.reshape(1, ATTN_WIDTH), k_norm_w.reshape(1, ATTN_WIDTH),
      inv_freq, seg)


def _attn_kernel(q_ref, k_ref, v_ref, o_ref, lse_ref, *, length, tq, tk, half):
    rel0 = (lax.broadcasted_iota(jnp.int32, (tq, tk), 1) - lax.broadcasted_iota(jnp.int32, (tq, tk), 0))
    lane = lax.broadcasted_iota(jnp.int32, (tq, LANES), 1)
    head_a = lane < HEAD_DIM

    def tile(i, carry):
        t0 = pl.multiple_of(i * tq, tq)
        ws = pl.multiple_of(jnp.clip(t0 - half, 0, length - tk), half)
        valid = jnp.abs(rel0 + (ws - t0)) <= half
        q = q_ref[0, pl.ds(t0, tq), :]
        k = k_ref[0, pl.ds(ws, tk), :]
        v = v_ref[0, pl.ds(ws, tk), :]
        for p in range(GROUP_WIDTH // LANES):
            qp = q[:, p * LANES:(p + 1) * LANES]
            kp = k[:, p * LANES:(p + 1) * LANES]
            vp = v[:, p * LANES:(p + 1) * LANES]
            outs, lses = [], []
            for sel in (head_a, jnp.logical_not(head_a)):
                qm = jnp.where(sel, qp, jnp.zeros_like(qp))
                s = lax.dot_general(qm, kp, (((1,), (1,)), ((), ())), preferred_element_type=F32)
                s = jnp.where(valid, s, NEG_INF)
                m = jnp.max(s, axis=1, keepdims=True)
                e = jnp.exp(s - m)
                den = jnp.sum(e, axis=1, keepdims=True)
                o = jnp.dot(e.astype(BF16), vp, preferred_element_type=F32)
                outs.append(o / den)
                lses.append(m + jnp.log(den))
            o_ref[0, pl.ds(t0, tq), p * LANES:(p + 1) * LANES] = jnp.where(head_a, outs[0], outs[1]).astype(o_ref.dtype)
            lse_ref[0, pl.ds(t0, tq), p * LANES:(p + 1) * LANES] = jnp.where(
                head_a, jnp.broadcast_to(lses[0], (tq, LANES)), jnp.broadcast_to(lses[1], (tq, LANES)))
        return carry

    lax.fori_loop(0, length // tq, tile, 0)


def _attention_group(q, k, v, group, window, dilation):
    bsz, s, width = q.shape
    length = s // dilation
    half = window // (2 * dilation)
    tq = min(ATTN_TQ, length)
    tk = min(tq + 2 * half, length)
    blocks_per_row = width // GROUP_WIDTH
    view = lambda t: t.reshape(bsz, length, dilation * width)
    in_spec = pl.BlockSpec((1, length, GROUP_WIDTH), lambda b, r: (b, 0, r * blocks_per_row + group))
    out_spec = pl.BlockSpec((1, length, GROUP_WIDTH), lambda b, r: (b, 0, r))
    o, lse = pl.pallas_call(
        functools.partial(_attn_kernel, length=length, tq=tq, tk=tk, half=half),
        grid=(bsz, dilation),
        in_specs=[in_spec] * 3,
        out_specs=[out_spec] * 2,
        out_shape=[jax.ShapeDtypeStruct((bsz, length, dilation * GROUP_WIDTH), BF16),
                   jax.ShapeDtypeStruct((bsz, length, dilation * GROUP_WIDTH), F32)],
        compiler_params=_params("arbitrary", "arbitrary"),
        name=f"attn_d{dilation}",
    )(view(q), view(k), view(v))
    return o.reshape(bsz, s, GROUP_WIDTH), lse.reshape(bsz, s, GROUP_WIDTH)


def _gelu_tanh(t):
    return 0.5 * t * (1.0 + jnp.tanh(0.7978845608028654 * (t + 0.044715 * t * t * t)))


def _lru_kernel(*refs, reverse, final, n_chunks):
    if final:
        (xr_ref, xp_ref, xn_ref, hf_ref, yr_ref, cw_ref, cb_ref, wa_ref, wx_ref, ba_ref, bx_ref, lam_ref,
         out_ref, ext_s, a_s, u_s, h_s, state_s) = refs
    else:
        (xr_ref, xp_ref, xn_ref, cw_ref, cb_ref, wa_ref, wx_ref, ba_ref, bx_ref, lam_ref,
         out_ref, ext_s, a_s, u_s, h_s, state_s) = refs
    nb, tc, ch = xr_ref.shape
    n_slabs = ch // LANES
    step = pl.program_id(1)
    chunk = (n_chunks - 1 - step) if reverse else step

    @pl.when(step == 0)
    def _():
        state_s[...] = jnp.zeros_like(state_s)

    has_prev = (chunk > 0).astype(F32)
    has_next = (chunk < n_chunks - 1).astype(F32)
    ext_s[:, 0:SUBLANES, :] = xp_ref[...].astype(F32)[:, HALO - SUBLANES:, :] * has_prev
    ext_s[:, SUBLANES:SUBLANES + tc, :] = xr_ref[...].astype(F32)
    ext_s[:, SUBLANES + tc:, :] = xn_ref[...].astype(F32)[:, :SUBLANES, :] * has_next

    lam = lam_ref[...]
    neg_c_softplus = -LRU_C * (jnp.maximum(-lam, 0.0) + jnp.log(1.0 + jnp.exp(-jnp.abs(lam))))
    pad_l = CONV_WIDTH // 2

    def coeffs(b, carry):
        xc = cb_ref[...]
        for j in range(CONV_WIDTH):
            lo = SUBLANES - pad_l + j
            xc = xc + cw_ref[j:j + 1, :] * ext_s[b, lo:lo + tc, :]
        xcb = xc.astype(BF16)
        ra, rx = [], []
        for kk in range(ch // MXU_DIM):
            piece = xcb[:, kk * MXU_DIM:(kk + 1) * MXU_DIM]
            ra.append(jnp.dot(piece, wa_ref[kk], preferred_element_type=F32))
            rx.append(jnp.dot(piece, wx_ref[kk], preferred_element_type=F32))
        r = _sigmoid(jnp.concatenate(ra, axis=1) + ba_ref[...])
        gate_i = _sigmoid(jnp.concatenate(rx, axis=1) + bx_ref[...])
        a = jnp.exp(r * neg_c_softplus)
        u = jnp.sqrt(1.0 - a * a) * (gate_i * xc)
        row0 = pl.multiple_of(b * LRU_PITCH, SUBLANES)
        for j in range(n_slabs):
            a_s[j, pl.ds(row0, tc), :] = a[:, j * LANES:(j + 1) * LANES]
            u_s[j, pl.ds(row0, tc), :] = u[:, j * LANES:(j + 1) * LANES]
        return carry

    lax.fori_loop(0, nb, coeffs, 0)

    def scan_step(i, hs):
        t = (tc - 1 - i) if reverse else i
        new = []
        for j in range(n_slabs):
            rows = pl.ds(t, nb, stride=LRU_PITCH)
            hj = a_s[j, rows, :] * hs[j] + u_s[j, rows, :]
            h_s[j, rows, :] = hj
            new.append(hj)
        return tuple(new)

    hs = lax.fori_loop(0, tc, scan_step, tuple(state_s[j] for j in range(n_slabs)), unroll=2)
    for j in range(n_slabs):
        state_s[j] = hs[j]

    def emit(b, carry):
        row0 = pl.multiple_of(b * LRU_PITCH, SUBLANES)
        hb = jnp.concatenate([h_s[j, pl.ds(row0, tc), :] for j in range(n_slabs)], axis=1)
        if final:
            hb = (hb + hf_ref[b].astype(F32)) * _gelu_tanh(yr_ref[b].astype(F32))
        out_ref[b] = hb.astype(out_ref.dtype)
        return carry

    lax.fori_loop(0, nb, emit, 0)


def _block_diag_tiles(w):
    per_tile = MXU_DIM // w.shape[1]
    n_tiles = w.shape[0] // per_tile
    w = w.reshape(n_tiles, per_tile, w.shape[1], w.shape[2])
    eye = jnp.eye(per_tile, dtype=w.dtype)
    tiles = jnp.einsum('tpkj,pq->tpkqj', w, eye)
    return tiles.reshape(n_tiles, MXU_DIM, MXU_DIM).astype(BF16)


def _lru_pass(xr, conv_w, conv_b, w_a, b_a, w_x, b_x, lam, *, reverse, h_fwd=None, yr=None):
    bsz, s, ch = xr.shape
    nb = min(LRU_NB, bsz)
    tc = min(LRU_TC, s)
    n_chunks = s // tc
    final = h_fwd is not None
    chunk_of = (lambda c: n_chunks - 1 - c) if reverse else (lambda c: c)
    per_halo = tc // HALO
    last_halo = s // HALO - 1
    main = pl.BlockSpec((nb, tc, ch), lambda g, c: (g, chunk_of(c), 0))
    prev = pl.BlockSpec((nb, HALO, ch), lambda g, c: (g, jnp.maximum(chunk_of(c) * per_halo - 1, 0), 0))
    nxt = pl.BlockSpec((nb, HALO, ch), lambda g, c: (g, jnp.minimum((chunk_of(c) + 1) * per_halo, last_halo), 0))
    const2 = lambda g, c: (0, 0)
    const3 = lambda g, c: (0, 0, 0)
    vec = pl.BlockSpec((1, ch), const2)
    tiles = pl.BlockSpec((ch // MXU_DIM, MXU_DIM, MXU_DIM), const3)
    acts = [xr, xr, xr] + ([h_fwd, yr] if final else [])
    act_specs = [main, prev, nxt] + ([main, main] if final else [])
    slab = pltpu.VMEM((ch // LANES, nb * LRU_PITCH, LANES), F32)
    return pl.pallas_call(
        functools.partial(_lru_kernel, reverse=reverse, final=final, n_chunks=n_chunks),
        grid=(bsz // nb, n_chunks),
        in_specs=act_specs + [pl.BlockSpec((CONV_WIDTH, ch), const2), vec, tiles, tiles, vec, vec, vec],
        out_specs=main,
        out_shape=jax.ShapeDtypeStruct((bsz, s, ch), BF16),
        scratch_shapes=[pltpu.VMEM((nb, tc + 2 * SUBLANES, ch), F32), slab, slab, slab,
                        pltpu.VMEM((ch // LANES, nb, LANES), F32)],
        compiler_params=_params("arbitrary", "arbitrary"),
        name="lru_bwd" if reverse else "lru_fwd",
    )(*acts, conv_w.reshape(CONV_WIDTH, ch), conv_b.reshape(1, ch), _block_diag_tiles(w_a), _block_diag_tiles(w_x),
      b_a.reshape(1, ch), b_x.reshape(1, ch), lam.reshape(1, ch))


def _merge_kernel(o0_ref, o1_ref, o2_ref, l0_ref, l1_ref, l2_ref, lru_ref, ga_ref, gl_ref, x_ref, gate_ref,
                  wab_ref, wlb_ref, wout_ref, nw_ref, shift_ref, scale_ref, wr_ref,
                  x1_ref, h2_ref, aff_ref):
    l0, l1, l2 = l0_ref[0], l1_ref[0], l2_ref[0]
    m = jnp.maximum(jnp.maximum(l0, l1), l2)
    e0, e1, e2 = jnp.exp(l0 - m), jnp.exp(l1 - m), jnp.exp(l2 - m)
    attn = (e0 * o0_ref[0].astype(F32) + e1 * o1_ref[0].astype(F32) + e2 * o2_ref[0].astype(F32)) / (e0 + e1 + e2)
    branch_a = jnp.dot(attn.astype(BF16), wab_ref[...], preferred_element_type=F32)
    branch_l = jnp.dot(lru_ref[0], wlb_ref[...], preferred_element_type=F32)
    merged = _sigmoid(ga_ref[0].astype(F32)) * branch_a + _sigmoid(gl_ref[0].astype(F32)) * branch_l
    mix = jnp.dot(merged.astype(BF16), wout_ref[...], preferred_element_type=F32)
    x1 = x_ref[0] + gate_ref[0] * mix
    x1_ref[0] = x1
    ms = jnp.mean(x1 * x1, axis=-1, keepdims=True)
    h2 = x1 * lax.rsqrt(ms + NORM_EPS) * nw_ref[...]
    h2 = h2 * (1.0 + scale_ref[0]) + shift_ref[0]
    h2_ref[0] = h2.astype(h2_ref.dtype)
    logits = lax.dot_general(wr_ref[...], h2, (((1,), (1,)), ((), ())), preferred_element_type=F32,
                             precision=lax.Precision.HIGHEST)
    z = jnp.exp(logits - jnp.max(logits, axis=0, keepdims=True))
    aff_ref[0] = z / jnp.sum(z, axis=0, keepdims=True)


def _merge(outs, lses, lru, ga, gl, x, gate1, w_ab, w_lb, w_out, norm_w, shift2, scale2, w_router):
    bsz, s, d = x.shape
    tm = min(ROW_TILE, s)
    n_exp = w_router.shape[1]
    row = lambda b, i: (b, i, 0)
    per_batch = lambda b, i: (b, 0, 0)
    const = lambda b, i: (0, 0)
    grp = pl.BlockSpec((1, tm, GROUP_WIDTH), row)
    wide = pl.BlockSpec((1, tm, d), row)
    mod = pl.BlockSpec((1, 1, d), per_batch)
    full = lambda a: pl.BlockSpec(a.shape, const)
    w_ab, w_lb, w_out = w_ab.astype(BF16), w_lb.astype(BF16), w_out.astype(BF16)
    w_rt = w_router.T
    return pl.pallas_call(
        _merge_kernel,
        grid=(bsz, s // tm),
        in_specs=[grp] * 6 + [wide] * 4 + [mod, full(w_ab), full(w_lb), full(w_out),
                                          pl.BlockSpec((1, d), const), mod, mod, full(w_rt)],
        out_specs=[wide, wide, pl.BlockSpec((1, n_exp, tm), lambda b, i: (b, 0, i))],
        out_shape=[jax.ShapeDtypeStruct((bsz, s, d), F32), jax.ShapeDtypeStruct((bsz, s, d), BF16),
                   jax.ShapeDtypeStruct((bsz, n_exp, s), F32)],
        compiler_params=_params("arbitrary", "arbitrary"),
        name="merge",
    )(*outs, *lses, lru, ga, gl, x, gate1.reshape(bsz, 1, d), w_ab, w_lb, w_out, norm_w.reshape(1, d),
      shift2.reshape(bsz, 1, d), scale2.reshape(bsz, 1, d), w_rt)


def _expert_kernel(x_ref, g_ref, wg_ref, wu_ref, wd_ref, y_ref):
    x = x_ref[0]
    gate = jnp.dot(x, wg_ref[0], preferred_element_type=F32)
    up = jnp.dot(x, wu_ref[0], preferred_element_type=F32)
    he = (gate * _sigmoid(gate) * up).astype(BF16)
    y_ref[0] = jnp.dot(he, wd_ref[0], preferred_element_type=F32) * g_ref[0]


def _experts(xe, gates, w_gate, w_up, w_down):
    n_exp, rows, d = xe.shape
    ff = w_gate.shape[2]
    tm = min(ROW_TILE, rows)
    row = lambda e, i: (e, i, 0)
    per_expert = lambda e, i: (e, 0, 0)
    return pl.pallas_call(
        _expert_kernel,
        grid=(n_exp, rows // tm),
        in_specs=[pl.BlockSpec((1, tm, d), row), pl.BlockSpec((1, tm, 1), row),
                  pl.BlockSpec((1, d, ff), per_expert), pl.BlockSpec((1, d, ff), per_expert),
                  pl.BlockSpec((1, ff, d), per_expert)],
        out_specs=pl.BlockSpec((1, tm, d), row),
        out_shape=jax.ShapeDtypeStruct((n_exp, rows, d), F32),
        compiler_params=_params("arbitrary", "arbitrary"),
        name="experts",
    )(xe, gates, w_gate.astype(BF16), w_up.astype(BF16), w_down.astype(BF16))


def _layer(x, c, positions, w_ada, b_ada, norm1_w, w_in, q_norm_w, k_norm_w, conv_w, conv_b, lru_w_a, lru_b_a,
           lru_w_x, lru_b_x, lru_lambda, w_attn_branch, w_lru_branch, w_out, norm2_w, w_router, w_gate, w_up,
           w_down):
    bsz, s, d = x.shape
    mod = _ada(c, w_ada, b_ada)
    shift1, scale1, gate1, shift2, scale2, gate2 = jnp.split(mod, 6, axis=-1)

    q, k, v, xr, yr, ga, gl = _inproj(x, positions, shift1, scale1, norm1_w, w_in, q_norm_w, k_norm_w)

    outs, lses = [], []
    for g, (window, dilation) in enumerate(ATTN_GROUPS):
        o, lse = _attention_group(q, k, v, g, window, dilation)
        outs.append(o)
        lses.append(lse)

    h_fwd = _lru_pass(xr, conv_w, conv_b, lru_w_a[0], lru_b_a[0], lru_w_x[0], lru_b_x[0], lru_lambda[0],
                      reverse=False)
    lru = _lru_pass(xr, conv_w, conv_b, lru_w_a[1], lru_b_a[1], lru_w_x[1], lru_b_x[1], lru_lambda[1],
                    reverse=True, h_fwd=h_fwd, yr=yr)

    x1, h2, aff = _merge(outs, lses, lru, ga, gl, x, gate1, w_attn_branch, w_lru_branch, w_out, norm2_w,
                         shift2, scale2, w_router)

    capacity = max(1, CAPACITY_FACTOR * s // N_EXPERTS)
    gates, idx = lax.top_k(aff, capacity)
    idx_e = jnp.swapaxes(idx, 0, 1)
    gates_e = jnp.swapaxes(gates, 0, 1)
    batch_ix = jnp.arange(bsz)[None, :, None]
    xe = h2[batch_ix, idx_e]
    ys = _experts(xe.reshape(N_EXPERTS, bsz * capacity, d), gates_e.reshape(N_EXPERTS, bsz * capacity, 1),
                  w_gate, w_up, w_down)
    ffn = jnp.zeros_like(x).at[batch_ix, idx_e].add(ys.reshape(N_EXPERTS, bsz, capacity, d))
    return x1 + gate2[:, None, :] * ffn


def kernel(x, c, positions, w_ada, b_ada, norm1_w, w_in, q_norm_w, k_norm_w, conv_w, conv_b, lru_w_a, lru_b_a,
           lru_w_x, lru_b_x, lru_lambda, w_attn_branch, w_lru_branch, w_out, norm2_w, w_router, w_gate, w_up,
           w_down):
    for l in range(w_ada.shape[0]):
        x = _layer(x, c, positions, w_ada[l], b_ada[l], norm1_w[l], w_in[l], q_norm_w[l], k_norm_w[l],
                   conv_w[l, :, 0, :], conv_b[l], lru_w_a[l], lru_b_a[l], lru_w_x[l], lru_b_x[l], lru_lambda[l],
                   w_attn_branch[l], w_lru_branch[l], w_out[l], norm2_w[l], w_router[l], w_gate[l], w_up[l],
                   w_down[l])
    return x
```

```python
import functools

import jax
import jax.numpy as jnp
from jax import lax
from jax.experimental import pallas as pl
from jax.experimental.pallas import tpu as pltpu

F32 = jnp.float32
BF16 = jnp.bfloat16

HEAD_DIM = 64
HEADS_PER_GROUP = 4
GROUP_WIDTH = HEADS_PER_GROUP * HEAD_DIM
ATTN_GROUPS = ((128, 1), (512, 4), (2048, 16))
ATTN_WIDTH = GROUP_WIDTH * len(ATTN_GROUPS)
LRU_BLOCKS = 16
LRU_C = 8.0
CONV_WIDTH = 4
N_EXPERTS = 16
CAPACITY_FACTOR = 2
ROPE_THETA = 10000.0
NORM_EPS = 1e-6
NEG_INF = -1e30

LANES = 128
SUBLANES = 8
MXU_DIM = 256
VMEM_LIMIT = 56 * 1024 * 1024

ROW_TILE = 512
ATTN_TQ = 128
LRU_TC = 128
LRU_NB = 8
LRU_PITCH = LRU_TC + 8
HALO = 16


def _sigmoid(t):
    return 1.0 / (1.0 + jnp.exp(-t))


def _params(*sem):
    return pltpu.CompilerParams(dimension_semantics=sem, vmem_limit_bytes=VMEM_LIMIT)


def _ada_kernel(c_ref, w_ref, b_ref, o_ref):
    c = c_ref[...]
    o_ref[...] = jnp.dot(c * _sigmoid(c), w_ref[...], preferred_element_type=F32,
                         precision=lax.Precision.HIGHEST) + b_ref[...]


def _ada(c, w, b):
    bsz, d = c.shape
    n = w.shape[1]
    tn = n // 4
    return pl.pallas_call(
        _ada_kernel,
        grid=(n // tn,),
        in_specs=[pl.BlockSpec((bsz, d), lambda j: (0, 0)),
                  pl.BlockSpec((d, tn), lambda j: (0, j)),
                  pl.BlockSpec((1, tn), lambda j: (0, j))],
        out_specs=pl.BlockSpec((bsz, tn), lambda j: (0, j)),
        out_shape=jax.ShapeDtypeStruct((bsz, n), F32),
        compiler_params=_params("arbitrary"),
        name="ada",
    )(c, w, b.reshape(1, n))


def _store_classes(val, out_ref, relayout_s, dilation):
    if dilation == 1:
        out_ref[0] = val.astype(out_ref.dtype)
        return
    n_tiles = GROUP_WIDTH // LANES
    rows = val.shape[0] // dilation
    for t in range(n_tiles):
        relayout_s[t] = val[:, t * LANES:(t + 1) * LANES]
    for r in range(dilation):
        for t in range(n_tiles):
            lo = r * GROUP_WIDTH + t * LANES
            out_ref[0, :, lo:lo + LANES] = relayout_s[t, pl.ds(r, rows, stride=dilation), :].astype(out_ref.dtype)


def _inproj_kernel(x_ref, pos_ref, shift_ref, scale_ref, nw_ref, w_ref, qnw_ref, knw_ref, invf_ref, seg_ref,
                   q0_ref, q1_ref, q2_ref, k0_ref, k1_ref, k2_ref, v0_ref, v1_ref, v2_ref,
                   xr_ref, yr_ref, ga_ref, gl_ref, relayout_s):
    x = x_ref[0]
    d = x.shape[1]
    ms = jnp.mean(x * x, axis=-1, keepdims=True)
    h = x * lax.rsqrt(ms + NORM_EPS) * nw_ref[...]
    hb = (h * (1.0 + scale_ref[0]) + shift_ref[0]).astype(BF16)

    ang = pos_ref[0].astype(F32) * invf_ref[...]
    cos = jnp.cos(ang)
    sin = jnp.sin(ang)
    lane = lax.broadcasted_iota(jnp.int32, (x.shape[0], GROUP_WIDTH), 1)
    first_half = (lane & (HEAD_DIM // 2)) == 0
    cos2 = jnp.concatenate([cos, cos], axis=1)
    sin2 = jnp.concatenate([sin, sin], axis=1)
    sin2 = jnp.where(first_half, -sin2, sin2)
    seg = seg_ref[...]

    def norm_rope(off, w_norm_ref, out_refs, post):
        for j, (_, dilation) in enumerate(ATTN_GROUPS):
            lo_c = j * GROUP_WIDTH
            t = jnp.dot(hb, w_ref[:, off + lo_c:off + lo_c + GROUP_WIDTH], preferred_element_type=F32)
            t2 = t * t
            hi = t2.astype(BF16)
            lo = (t2 - hi.astype(F32)).astype(BF16)
            msq = (jnp.dot(hi, seg, preferred_element_type=F32)
                   + jnp.dot(lo, seg, preferred_element_type=F32)) * (1.0 / HEAD_DIM)
            y = t * lax.rsqrt(msq + NORM_EPS) * w_norm_ref[:, lo_c:lo_c + GROUP_WIDTH]
            rot = jnp.where(first_half, pltpu.roll(y, GROUP_WIDTH - HEAD_DIM // 2, 1),
                            pltpu.roll(y, HEAD_DIM // 2, 1))
            r = y * cos2 + rot * sin2
            _store_classes(r * post, out_refs[j], relayout_s, dilation)

    norm_rope(0, qnw_ref, (q0_ref, q1_ref, q2_ref), HEAD_DIM ** -0.5)
    norm_rope(ATTN_WIDTH, knw_ref, (k0_ref, k1_ref, k2_ref), 1.0)
    off = 2 * ATTN_WIDTH
    for v_ref, (_, dilation) in zip((v0_ref, v1_ref, v2_ref), ATTN_GROUPS):
        v = jnp.dot(hb, w_ref[:, off:off + GROUP_WIDTH], preferred_element_type=F32)
        _store_classes(v, v_ref, relayout_s, dilation)
        off += GROUP_WIDTH
    for out_ref in (xr_ref, yr_ref, ga_ref, gl_ref):
        out_ref[0] = jnp.dot(hb, w_ref[:, off:off + d], preferred_element_type=F32).astype(out_ref.dtype)
        off += d


def _inproj(x, positions, shift, scale, norm_w, w_in, q_norm_w, k_norm_w):
    bsz, s, d = x.shape
    tm = min(ROW_TILE, s)
    half = HEAD_DIM // 2
    inv_freq = ROPE_THETA ** (-jnp.arange(half, dtype=F32) / half)
    inv_freq = jnp.tile(inv_freq, LANES // half).reshape(1, LANES)
    head_of = jnp.arange(GROUP_WIDTH) // HEAD_DIM
    seg = (head_of[:, None] == head_of[None, :]).astype(BF16)
    row = lambda b, i: (b, i, 0)
    per_batch = lambda b, i: (b, 0, 0)
    const = lambda b, i: (0, 0)
    wide = [jax.ShapeDtypeStruct((bsz, s, d), BF16)] * 4
    narrow = [jax.ShapeDtypeStruct((bsz, s // dil, dil * GROUP_WIDTH), BF16) for _, dil in ATTN_GROUPS] * 3
    narrow_specs = [pl.BlockSpec((1, tm // dil, dil * GROUP_WIDTH), row) for _, dil in ATTN_GROUPS] * 3
    return pl.pallas_call(
        _inproj_kernel,
        grid=(bsz, s // tm),
        in_specs=[pl.BlockSpec((1, tm, d), row),
                  pl.BlockSpec((1, tm, 1), row),
                  pl.BlockSpec((1, 1, d), per_batch),
                  pl.BlockSpec((1, 1, d), per_batch),
                  pl.BlockSpec((1, d), const),
                  pl.BlockSpec(w_in.shape, const),
                  pl.BlockSpec((1, ATTN_WIDTH), const),
                  pl.BlockSpec((1, ATTN_WIDTH), const),
                  pl.BlockSpec((1, LANES), const),
                  pl.BlockSpec((GROUP_WIDTH, GROUP_WIDTH), const)],
        out_specs=narrow_specs + [pl.BlockSpec((1, tm, d), row)] * 4,
        out_shape=narrow + wide,
        scratch_shapes=[pltpu.VMEM((GROUP_WIDTH // LANES, tm, LANES), F32)],
        compiler_params=_params("arbitrary", "arbitrary"),
        name="inproj",
    )(x, positions.reshape(bsz, s, 1), shift.reshape(bsz, 1, d), scale.reshape(bsz, 1, d),
      norm_w.reshape(1, d), w_in.astype(BF16), q_norm_w.reshape(1, ATTN_WIDTH), k_norm_w.reshape(1, ATTN_WIDTH),
      inv_freq, seg)


def _attn_kernel(q_ref, k_ref, v_ref, o_ref, lse_ref, *, length, tq, tk, half):
    rel0 = (lax.broadcasted_iota(jnp.int32, (tq, tk), 1) - lax.broadcasted_iota(jnp.int32, (tq, tk), 0))
    lane = lax.broadcasted_iota(jnp.int32, (tq, LANES), 1)
    head_a = lane < HEAD_DIM

    def tile(i, carry):
        t0 = pl.multiple_of(i * tq, tq)
        ws = pl.multiple_of(jnp.clip(t0 - half, 0, length - tk), half)
        valid = jnp.abs(rel0 + (ws - t0)) <= half
        q = q_ref[0, pl.ds(t0, tq), :]
        k = k_ref[0, pl.ds(ws, tk), :]
        v = v_ref[0, pl.ds(ws, tk), :]
        for p in range(GROUP_WIDTH // LANES):
            qp = q[:, p * LANES:(p + 1) * LANES]
            kp = k[:, p * LANES:(p + 1) * LANES]
            vp = v[:, p * LANES:(p + 1) * LANES]
            outs, lses = [], []
            for sel in (head_a, jnp.logical_not(head_a)):
                qm = jnp.where(sel, qp, jnp.zeros_like(qp))
                s = lax.dot_general(qm, kp, (((1,), (1,)), ((), ())), preferred_element_type=F32)
                s = jnp.where(valid, s, NEG_INF)
                m = jnp.max(s, axis=1, keepdims=True)
                e = jnp.exp(s - m)
                den = jnp.sum(e, axis=1, keepdims=True)
                o = jnp.dot(e.astype(BF16), vp, preferred_element_type=F32)
                outs.append(o / den)
                lses.append(m + jnp.log(den))
            o_ref[0, pl.ds(t0, tq), p * LANES:(p + 1) * LANES] = jnp.where(head_a, outs[0], outs[1]).astype(o_ref.dtype)
            lse_ref[0, pl.ds(t0, tq), p * LANES:(p + 1) * LANES] = jnp.where(
                head_a, jnp.broadcast_to(lses[0], (tq, LANES)), jnp.broadcast_to(lses[1], (tq, LANES)))
        return carry

    lax.fori_loop(0, length // tq, tile, 0)


def _attention_group(q, k, v, window, dilation):
    bsz, length, _ = q.shape
    half = window // (2 * dilation)
    tq = min(ATTN_TQ, length)
    tk = min(tq + 2 * half, length)
    spec = pl.BlockSpec((1, length, GROUP_WIDTH), lambda b, r: (b, 0, r))
    return pl.pallas_call(
        functools.partial(_attn_kernel, length=length, tq=tq, tk=tk, half=half),
        grid=(bsz, dilation),
        in_specs=[spec] * 3,
        out_specs=[spec] * 2,
        out_shape=[jax.ShapeDtypeStruct(q.shape, BF16), jax.ShapeDtypeStruct(q.shape, F32)],
        compiler_params=_params("arbitrary", "arbitrary"),
        name=f"attn_d{dilation}",
    )(q, k, v)


def _gelu_tanh(t):
    return 0.5 * t * (1.0 + jnp.tanh(0.7978845608028654 * (t + 0.044715 * t * t * t)))


def _lru_kernel(*refs, reverse, final, n_chunks):
    if final:
        (xr_ref, xp_ref, xn_ref, hf_ref, yr_ref, cw_ref, cb_ref, wa_ref, wx_ref, ba_ref, bx_ref, lam_ref,
         out_ref, ext_s, a_s, u_s, h_s, state_s) = refs
    else:
        (xr_ref, xp_ref, xn_ref, cw_ref, cb_ref, wa_ref, wx_ref, ba_ref, bx_ref, lam_ref,
         out_ref, ext_s, a_s, u_s, h_s, state_s) = refs
    nb, tc, ch = xr_ref.shape
    n_slabs = ch // LANES
    step = pl.program_id(1)
    chunk = (n_chunks - 1 - step) if reverse else step

    @pl.when(step == 0)
    def _():
        state_s[...] = jnp.zeros_like(state_s)

    has_prev = (chunk > 0).astype(F32)
    has_next = (chunk < n_chunks - 1).astype(F32)
    ext_s[:, 0:SUBLANES, :] = xp_ref[...].astype(F32)[:, HALO - SUBLANES:, :] * has_prev
    ext_s[:, SUBLANES:SUBLANES + tc, :] = xr_ref[...].astype(F32)
    ext_s[:, SUBLANES + tc:, :] = xn_ref[...].astype(F32)[:, :SUBLANES, :] * has_next

    lam = lam_ref[...]
    neg_c_softplus = -LRU_C * (jnp.maximum(-lam, 0.0) + jnp.log(1.0 + jnp.exp(-jnp.abs(lam))))
    pad_l = CONV_WIDTH // 2

    def coeffs(b, carry):
        xc = cb_ref[...]
        for j in range(CONV_WIDTH):
            lo = SUBLANES - pad_l + j
            xc = xc + cw_ref[j:j + 1, :] * ext_s[b, lo:lo + tc, :]
        xcb = xc.astype(BF16)
        ra, rx = [], []
        for kk in range(ch // MXU_DIM):
            piece = xcb[:, kk * MXU_DIM:(kk + 1) * MXU_DIM]
            ra.append(jnp.dot(piece, wa_ref[kk], preferred_element_type=F32))
            rx.append(jnp.dot(piece, wx_ref[kk], preferred_element_type=F32))
        r = _sigmoid(jnp.concatenate(ra, axis=1) + ba_ref[...])
        gate_i = _sigmoid(jnp.concatenate(rx, axis=1) + bx_ref[...])
        a = jnp.exp(r * neg_c_softplus)
        u = jnp.sqrt(1.0 - a * a) * (gate_i * xc)
        row0 = pl.multiple_of(b * LRU_PITCH, SUBLANES)
        for j in range(n_slabs):
            a_s[j, pl.ds(row0, tc), :] = a[:, j * LANES:(j + 1) * LANES]
            u_s[j, pl.ds(row0, tc), :] = u[:, j * LANES:(j + 1) * LANES]
        return carry

    lax.fori_loop(0, nb, coeffs, 0)

    def scan_step(i, hs):
        t = (tc - 1 - i) if reverse else i
        new = []
        for j in range(n_slabs):
            rows = pl.ds(t, nb, stride=LRU_PITCH)
            hj = a_s[j, rows, :] * hs[j] + u_s[j, rows, :]
            h_s[j, rows, :] = hj
            new.append(hj)
        return tuple(new)

    hs = lax.fori_loop(0, tc, scan_step, tuple(state_s[j] for j in range(n_slabs)), unroll=2)
    for j in range(n_slabs):
        state_s[j] = hs[j]

    def emit(b, carry):
        row0 = pl.multiple_of(b * LRU_PITCH, SUBLANES)
        hb = jnp.concatenate([h_s[j, pl.ds(row0, tc), :] for j in range(n_slabs)], axis=1)
        if final:
            hb = (hb + hf_ref[b].astype(F32)) * _gelu_tanh(yr_ref[b].astype(F32))
        out_ref[b] = hb.astype(out_ref.dtype)
        return carry

    lax.fori_loop(0, nb, emit, 0)


def _block_diag_tiles(w):
    per_tile = MXU_DIM // w.shape[1]
    n_tiles = w.shape[0] // per_tile
    w = w.reshape(n_tiles, per_tile, w.shape[1], w.shape[2])
    eye = jnp.eye(per_tile, dtype=w.dtype)
    tiles = jnp.einsum('tpkj,pq->tpkqj', w, eye)
    return tiles.reshape(n_tiles, MXU_DIM, MXU_DIM).astype(BF16)


def _lru_pass(xr, conv_w, conv_b, w_a, b_a, w_x, b_x, lam, *, reverse, h_fwd=None, yr=None):
    bsz, s, ch = xr.shape
    nb = min(LRU_NB, bsz)
    tc = min(LRU_TC, s)
    n_chunks = s // tc
    final = h_fwd is not None
    chunk_of = (lambda c: n_chunks - 1 - c) if reverse else (lambda c: c)
    per_halo = tc // HALO
    last_halo = s // HALO - 1
    main = pl.BlockSpec((nb, tc, ch), lambda g, c: (g, chunk_of(c), 0))
    prev = pl.BlockSpec((nb, HALO, ch), lambda g, c: (g, jnp.maximum(chunk_of(c) * per_halo - 1, 0), 0))
    nxt = pl.BlockSpec((nb, HALO, ch), lambda g, c: (g, jnp.minimum((chunk_of(c) + 1) * per_halo, last_halo), 0))
    const2 = lambda g, c: (0, 0)
    const3 = lambda g, c: (0, 0, 0)
    vec = pl.BlockSpec((1, ch), const2)
    tiles = pl.BlockSpec((ch // MXU_DIM, MXU_DIM, MXU_DIM), const3)
    acts = [xr, xr, xr] + ([h_fwd, yr] if final else [])
    act_specs = [main, prev, nxt] + ([main, main] if final else [])
    slab = pltpu.VMEM((ch // LANES, nb * LRU_PITCH, LANES), F32)
    return pl.pallas_call(
        functools.partial(_lru_kernel, reverse=reverse, final=final, n_chunks=n_chunks),
        grid=(bsz // nb, n_chunks),
        in_specs=act_specs + [pl.BlockSpec((CONV_WIDTH, ch), const2), vec, tiles, tiles, vec, vec, vec],
        out_specs=main,
        out_shape=jax.ShapeDtypeStruct((bsz, s, ch), BF16),
        scratch_shapes=[pltpu.VMEM((nb, tc + 2 * SUBLANES, ch), F32), slab, slab, slab,
                        pltpu.VMEM((ch // LANES, nb, LANES), F32)],
        compiler_params=_params("arbitrary", "arbitrary"),
        name="lru_bwd" if reverse else "lru_fwd",
    )(*acts, conv_w.reshape(CONV_WIDTH, ch), conv_b.reshape(1, ch), _block_diag_tiles(w_a), _block_diag_tiles(w_x),
      b_a.reshape(1, ch), b_x.reshape(1, ch), lam.reshape(1, ch))


def _load_classes(in_ref, relayout_s, dilation):
    if dilation == 1:
        return in_ref[0].astype(F32)
    n_tiles = GROUP_WIDTH // LANES
    rows = in_ref.shape[1]
    for r in range(dilation):
        for t in range(n_tiles):
            lo = r * GROUP_WIDTH + t * LANES
            relayout_s[t, pl.ds(r, rows, stride=dilation), :] = in_ref[0, :, lo:lo + LANES].astype(F32)
    return jnp.concatenate([relayout_s[t] for t in range(n_tiles)], axis=1)


def _merge_kernel(o0_ref, o1_ref, o2_ref, l0_ref, l1_ref, l2_ref, lru_ref, ga_ref, gl_ref, x_ref, gate_ref,
                  wab_ref, wlb_ref, wout_ref, nw_ref, shift_ref, scale_ref, wr_ref,
                  x1_ref, h2_ref, aff_ref, so1_s, so2_s, sl1_s, sl2_s):
    dil = [dilation for _, dilation in ATTN_GROUPS]
    l0 = _load_classes(l0_ref, None, dil[0])
    l1 = _load_classes(l1_ref, sl1_s, dil[1])
    l2 = _load_classes(l2_ref, sl2_s, dil[2])
    o0 = _load_classes(o0_ref, None, dil[0])
    o1 = _load_classes(o1_ref, so1_s, dil[1])
    o2 = _load_classes(o2_ref, so2_s, dil[2])
    m = jnp.maximum(jnp.maximum(l0, l1), l2)
    e0, e1, e2 = jnp.exp(l0 - m), jnp.exp(l1 - m), jnp.exp(l2 - m)
    attn = (e0 * o0 + e1 * o1 + e2 * o2) / (e0 + e1 + e2)
    branch_a = jnp.dot(attn.astype(BF16), wab_ref[...], preferred_element_type=F32)
    branch_l = jnp.dot(lru_ref[0], wlb_ref[...], preferred_element_type=F32)
    merged = _sigmoid(ga_ref[0].astype(F32)) * branch_a + _sigmoid(gl_ref[0].astype(F32)) * branch_l
    mix = jnp.dot(merged.astype(BF16), wout_ref[...], preferred_element_type=F32)
    x1 = x_ref[0] + gate_ref[0] * mix
    x1_ref[0] = x1
    ms = jnp.mean(x1 * x1, axis=-1, keepdims=True)
    h2 = x1 * lax.rsqrt(ms + NORM_EPS) * nw_ref[...]
    h2 = h2 * (1.0 + scale_ref[0]) + shift_ref[0]
    h2_ref[0] = h2.astype(h2_ref.dtype)
    logits = lax.dot_general(wr_ref[...], h2, (((1,), (1,)), ((), ())), preferred_element_type=F32,
                             precision=lax.Precision.HIGHEST)
    z = jnp.exp(logits - jnp.max(logits, axis=0, keepdims=True))
    aff_ref[0] = z / jnp.sum(z, axis=0, keepdims=True)


def _merge(outs, lses, lru, ga, gl, x, gate1, w_ab, w_lb, w_out, norm_w, shift2, scale2, w_router):
    bsz, s, d = x.shape
    tm = min(ROW_TILE, s)
    n_exp = w_router.shape[1]
    row = lambda b, i: (b, i, 0)
    per_batch = lambda b, i: (b, 0, 0)
    const = lambda b, i: (0, 0)
    grp = [pl.BlockSpec((1, tm // dil, dil * GROUP_WIDTH), row) for _, dil in ATTN_GROUPS]
    wide = pl.BlockSpec((1, tm, d), row)
    mod = pl.BlockSpec((1, 1, d), per_batch)
    full = lambda a: pl.BlockSpec(a.shape, const)
    w_ab, w_lb, w_out = w_ab.astype(BF16), w_lb.astype(BF16), w_out.astype(BF16)
    w_rt = w_router.T
    return pl.pallas_call(
        _merge_kernel,
        grid=(bsz, s // tm),
        in_specs=grp * 2 + [wide] * 4 + [mod, full(w_ab), full(w_lb), full(w_out),
                                          pl.BlockSpec((1, d), const), mod, mod, full(w_rt)],
        out_specs=[wide, wide, pl.BlockSpec((1, n_exp, tm), lambda b, i: (b, 0, i))],
        out_shape=[jax.ShapeDtypeStruct((bsz, s, d), F32), jax.ShapeDtypeStruct((bsz, s, d), BF16),
                   jax.ShapeDtypeStruct((bsz, n_exp, s), F32)],
        scratch_shapes=[pltpu.VMEM((GROUP_WIDTH // LANES, tm, LANES), F32)] * 4,
        compiler_params=_params("arbitrary", "arbitrary"),
        name="merge",
    )(*outs, *lses, lru, ga, gl, x, gate1.reshape(bsz, 1, d), w_ab, w_lb, w_out, norm_w.reshape(1, d),
      shift2.reshape(bsz, 1, d), scale2.reshape(bsz, 1, d), w_rt)


def _expert_kernel(x_ref, g_ref, wg_ref, wu_ref, wd_ref, y_ref):
    x = x_ref[0]
    gate = jnp.dot(x, wg_ref[0], preferred_element_type=F32)
    up = jnp.dot(x, wu_ref[0], preferred_element_type=F32)
    he = (gate * _sigmoid(gate) * up).astype(BF16)
    y_ref[0] = jnp.dot(he, wd_ref[0], preferred_element_type=F32) * g_ref[0]


def _experts(xe, gates, w_gate, w_up, w_down):
    n_exp, rows, d = xe.shape
    ff = w_gate.shape[2]
    tm = min(ROW_TILE, rows)
    row = lambda e, i: (e, i, 0)
    per_expert = lambda e, i: (e, 0, 0)
    return pl.pallas_call(
        _expert_kernel,
        grid=(n_exp, rows // tm),
        in_specs=[pl.BlockSpec((1, tm, d), row), pl.BlockSpec((1, tm, 1), row),
                  pl.BlockSpec((1, d, ff), per_expert), pl.BlockSpec((1, d, ff), per_expert),
                  pl.BlockSpec((1, ff, d), per_expert)],
        out_specs=pl.BlockSpec((1, tm, d), row),
        out_shape=jax.ShapeDtypeStruct((n_exp, rows, d), F32),
        compiler_params=_params("arbitrary", "arbitrary"),
        name="experts",
    )(xe, gates, w_gate.astype(BF16), w_up.astype(BF16), w_down.astype(BF16))


def _layer(x, c, positions, w_ada, b_ada, norm1_w, w_in, q_norm_w, k_norm_w, conv_w, conv_b, lru_w_a, lru_b_a,
           lru_w_x, lru_b_x, lru_lambda, w_attn_branch, w_lru_branch, w_out, norm2_w, w_router, w_gate, w_up,
           w_down):
    bsz, s, d = x.shape
    mod = _ada(c, w_ada, b_ada)
    shift1, scale1, gate1, shift2, scale2, gate2 = jnp.split(mod, 6, axis=-1)

    (q0, q1, q2, k0, k1, k2, v0, v1, v2, xr, yr, ga, gl) = _inproj(x, positions, shift1, scale1, norm1_w, w_in,
                                                                  q_norm_w, k_norm_w)

    outs, lses = [], []
    for (window, dilation), q, k, v in zip(ATTN_GROUPS, (q0, q1, q2), (k0, k1, k2), (v0, v1, v2)):
        o, lse = _attention_group(q, k, v, window, dilation)
        outs.append(o)
        lses.append(lse)

    h_fwd = _lru_pass(xr, conv_w, conv_b, lru_w_a[0], lru_b_a[0], lru_w_x[0], lru_b_x[0], lru_lambda[0],
                      reverse=False)
    lru = _lru_pass(xr, conv_w, conv_b, lru_w_a[1], lru_b_a[1], lru_w_x[1], lru_b_x[1], lru_lambda[1],
                    reverse=True, h_fwd=h_fwd, yr=yr)

    x1, h2, aff = _merge(outs, lses, lru, ga, gl, x, gate1, w_attn_branch, w_lru_branch, w_out, norm2_w,
                         shift2, scale2, w_router)

    capacity = max(1, CAPACITY_FACTOR * s // N_EXPERTS)
    gates, idx = lax.top_k(aff, capacity)
    idx_e = jnp.swapaxes(idx, 0, 1)
    gates_e = jnp.swapaxes(gates, 0, 1)
    batch_ix = jnp.arange(bsz)[None, :, None]
    xe = h2[batch_ix, idx_e]
    ys = _experts(xe.reshape(N_EXPERTS, bsz * capacity, d), gates_e.reshape(N_EXPERTS, bsz * capacity, 1),
                  w_gate, w_up, w_down)
    ffn = jnp.zeros_like(x).at[batch_ix, idx_e].add(ys.reshape(N_EXPERTS, bsz, capacity, d))
    return x1 + gate2[:, None, :] * ffn


def kernel(x, c, positions, w_ada, b_ada, norm1_w, w_in, q_norm_w, k_norm_w, conv_w, conv_b, lru_w_a, lru_b_a,
           lru_w_x, lru_b_x, lru_lambda, w_attn_branch, w_lru_branch, w_out, norm2_w, w_router, w_gate, w_up,
           w_down):
    for l in range(w_ada.shape[0]):
        x = _layer(x, c, positions, w_ada[l], b_ada[l], norm1_w[l], w_in[l], q_norm_w[l], k_norm_w[l],
                   conv_w[l, :, 0, :], conv_b[l], lru_w_a[l], lru_b_a[l], lru_w_x[l], lru_b_x[l], lru_lambda[l],
                   w_attn_branch[l], w_lru_branch[l], w_out[l], norm2_w[l], w_router[l], w_gate[l], w_up[l],
                   w_down[l])
    return x
```

```python
import functools

import jax
import jax.numpy as jnp
from jax import lax
from jax.experimental import pallas as pl
from jax.experimental.pallas import tpu as pltpu

F32 = jnp.float32
BF16 = jnp.bfloat16

HEAD_DIM = 64
HEADS_PER_GROUP = 4
GROUP_WIDTH = HEADS_PER_GROUP * HEAD_DIM
ATTN_GROUPS = ((128, 1), (512, 4), (2048, 16))
ATTN_WIDTH = GROUP_WIDTH * len(ATTN_GROUPS)
LRU_BLOCKS = 16
LRU_C = 8.0
CONV_WIDTH = 4
N_EXPERTS = 16
CAPACITY_FACTOR = 2
ROPE_THETA = 10000.0
NORM_EPS = 1e-6
NEG_INF = -1e30
LOG2_E = 1.4426950408889634
TINY = 1e-30

LANES = 128
SUBLANES = 8
MXU_DIM = 256
VMEM_LIMIT = 56 * 1024 * 1024

ROW_TILE = 512
SUB_TILES = 2
ATTN_TQ = 128
LRU_TC = 128
LRU_NB = 8
LRU_PITCH = LRU_TC + 8
HALO = 16
ROUTE_CHUNK = 256
ROUTE_WINDOW = 64
ROUTE_ALIGN = 16
EXPERT_GROUP = 8


def _sigmoid(t):
    return 0.5 * jnp.tanh(0.5 * t) + 0.5


def _params(*sem):
    return pltpu.CompilerParams(dimension_semantics=sem, vmem_limit_bytes=VMEM_LIMIT)


def _ada_kernel(c_ref, w_ref, b_ref, o_ref):
    c = c_ref[...]
    o_ref[...] = jnp.dot(c * _sigmoid(c), w_ref[...], preferred_element_type=F32,
                         precision=lax.Precision.HIGHEST) + b_ref[...]


def _ada(c, w, b):
    bsz, d = c.shape
    n = w.shape[1]
    tn = n // 4
    return pl.pallas_call(
        _ada_kernel,
        grid=(n // tn,),
        in_specs=[pl.BlockSpec((bsz, d), lambda j: (0, 0)),
                  pl.BlockSpec((d, tn), lambda j: (0, j)),
                  pl.BlockSpec((1, tn), lambda j: (0, j))],
        out_specs=pl.BlockSpec((bsz, tn), lambda j: (0, j)),
        out_shape=jax.ShapeDtypeStruct((bsz, n), F32),
        compiler_params=_params("arbitrary"),
        name="ada",
    )(c, w, b.reshape(1, n))


def _store_classes(val, out_ref, row0, relayout_s, dilation):
    n = val.shape[0]
    if dilation == 1:
        out_ref[0, row0:row0 + n, :] = val.astype(out_ref.dtype)
        return
    n_tiles = GROUP_WIDTH // LANES
    rows = n // dilation
    for t in range(n_tiles):
        relayout_s[t] = val[:, t * LANES:(t + 1) * LANES]
    for r in range(dilation):
        for t in range(n_tiles):
            lo = r * GROUP_WIDTH + t * LANES
            out_ref[0, row0 // dilation:row0 // dilation + rows, lo:lo + LANES] = (
                relayout_s[t, pl.ds(r, rows, stride=dilation), :].astype(out_ref.dtype))


def _inproj_kernel(x_ref, pos_ref, shift_ref, scale_ref, nw_ref, w_ref, qnw_ref, knw_ref, invf_ref, seg_ref,
                   q0_ref, q1_ref, q2_ref, k0_ref, k1_ref, k2_ref, v0_ref, v1_ref, v2_ref,
                   xr_ref, yr_ref, ga_ref, gl_ref, relayout_s):
    tm, d = x_ref.shape[1:]
    sub = tm // SUB_TILES
    lane = lax.broadcasted_iota(jnp.int32, (sub, GROUP_WIDTH), 1)
    first_half = (lane & (HEAD_DIM // 2)) == 0
    seg = seg_ref[...]

    for part in range(SUB_TILES):
        row0 = part * sub
        x = x_ref[0, row0:row0 + sub, :]
        ms = jnp.mean(x * x, axis=-1, keepdims=True)
        h = x * lax.rsqrt(ms + NORM_EPS) * nw_ref[...]
        hb = (h * (1.0 + scale_ref[0]) + shift_ref[0]).astype(BF16)

        ang = pos_ref[0, row0:row0 + sub, :].astype(F32) * invf_ref[...]
        cos = jnp.cos(ang)
        sin = jnp.sin(ang)
        cos2 = jnp.concatenate([cos, cos], axis=1)
        sin2 = jnp.concatenate([sin, sin], axis=1)
        sin2 = jnp.where(first_half, -sin2, sin2)
        scratch = relayout_s.at[part]

        def norm_rope(off, w_norm_ref, out_refs, post):
            for j, (_, dilation) in enumerate(ATTN_GROUPS):
                lo_c = j * GROUP_WIDTH
                t = jnp.dot(hb, w_ref[:, off + lo_c:off + lo_c + GROUP_WIDTH], preferred_element_type=F32)
                t2 = t * t
                hi = t2.astype(BF16)
                lo = (t2 - hi.astype(F32)).astype(BF16)
                msq = (jnp.dot(hi, seg, preferred_element_type=F32)
                       + jnp.dot(lo, seg, preferred_element_type=F32)) * (1.0 / HEAD_DIM)
                y = t * lax.rsqrt(msq + NORM_EPS) * w_norm_ref[:, lo_c:lo_c + GROUP_WIDTH]
                rot = jnp.where(first_half, pltpu.roll(y, GROUP_WIDTH - HEAD_DIM // 2, 1),
                                pltpu.roll(y, HEAD_DIM // 2, 1))
                r = y * cos2 + rot * sin2
                _store_classes(r * post, out_refs[j], row0, scratch, dilation)

        norm_rope(0, qnw_ref, (q0_ref, q1_ref, q2_ref), HEAD_DIM ** -0.5)
        norm_rope(ATTN_WIDTH, knw_ref, (k0_ref, k1_ref, k2_ref), 1.0)
        off = 2 * ATTN_WIDTH
        for v_ref, (_, dilation) in zip((v0_ref, v1_ref, v2_ref), ATTN_GROUPS):
            v = jnp.dot(hb, w_ref[:, off:off + GROUP_WIDTH], preferred_element_type=F32)
            _store_classes(v, v_ref, row0, scratch, dilation)
            off += GROUP_WIDTH
        for out_ref in (xr_ref, yr_ref, ga_ref, gl_ref):
            out_ref[0, row0:row0 + sub, :] = jnp.dot(hb, w_ref[:, off:off + d],
                                                     preferred_element_type=F32).astype(out_ref.dtype)
            off += d


def _inproj(x, positions, shift, scale, norm_w, w_in, q_norm_w, k_norm_w):
    bsz, s, d = x.shape
    tm = min(ROW_TILE, s)
    half = HEAD_DIM // 2
    inv_freq = ROPE_THETA ** (-jnp.arange(half, dtype=F32) / half)
    inv_freq = jnp.tile(inv_freq, LANES // half).reshape(1, LANES)
    head_of = jnp.arange(GROUP_WIDTH) // HEAD_DIM
    seg = (head_of[:, None] == head_of[None, :]).astype(BF16)
    row = lambda b, i: (b, i, 0)
    per_batch = lambda b, i: (b, 0, 0)
    const = lambda b, i: (0, 0)
    wide = [jax.ShapeDtypeStruct((bsz, s, d), BF16)] * 4
    narrow = [jax.ShapeDtypeStruct((bsz, s // dil, dil * GROUP_WIDTH), BF16) for _, dil in ATTN_GROUPS] * 3
    narrow_specs = [pl.BlockSpec((1, tm // dil, dil * GROUP_WIDTH), row) for _, dil in ATTN_GROUPS] * 3
    return pl.pallas_call(
        _inproj_kernel,
        grid=(bsz, s // tm),
        in_specs=[pl.BlockSpec((1, tm, d), row),
                  pl.BlockSpec((1, tm, 1), row),
                  pl.BlockSpec((1, 1, d), per_batch),
                  pl.BlockSpec((1, 1, d), per_batch),
                  pl.BlockSpec((1, d), const),
                  pl.BlockSpec(w_in.shape, const),
                  pl.BlockSpec((1, ATTN_WIDTH), const),
                  pl.BlockSpec((1, ATTN_WIDTH), const),
                  pl.BlockSpec((1, LANES), const),
                  pl.BlockSpec((GROUP_WIDTH, GROUP_WIDTH), const)],
        out_specs=narrow_specs + [pl.BlockSpec((1, tm, d), row)] * 4,
        out_shape=narrow + wide,
        scratch_shapes=[pltpu.VMEM((SUB_TILES, GROUP_WIDTH // LANES, tm // SUB_TILES, LANES), F32)],
        compiler_params=_params("arbitrary", "arbitrary"),
        name="inproj",
    )(x, positions.reshape(bsz, s, 1), shift.reshape(bsz, 1, d), scale.reshape(bsz, 1, d),
      norm_w.reshape(1, d), w_in.astype(BF16), q_norm_w.reshape(1, ATTN_WIDTH), k_norm_w.reshape(1, ATTN_WIDTH),
      inv_freq, seg)


def _attn_kernel(q_ref, k_ref, v_ref, o_ref, lse_ref, *, length, tq, tk, half):
    rel0 = (lax.broadcasted_iota(jnp.int32, (tq, tk), 1) - lax.broadcasted_iota(jnp.int32, (tq, tk), 0))
    lane = lax.broadcasted_iota(jnp.int32, (tq, LANES), 1)
    head_a = lane < HEAD_DIM

    def tile(i, carry):
        t0 = pl.multiple_of(i * tq, tq)
        ws = pl.multiple_of(jnp.clip(t0 - half, 0, length - tk), half)
        valid = jnp.abs(rel0 + (ws - t0)) <= half
        q = q_ref[0, pl.ds(t0, tq), :]
        k = k_ref[0, pl.ds(ws, tk), :]
        v = v_ref[0, pl.ds(ws, tk), :]
        for p in range(GROUP_WIDTH // LANES):
            qp = q[:, p * LANES:(p + 1) * LANES]
            kp = k[:, p * LANES:(p + 1) * LANES]
            vp = v[:, p * LANES:(p + 1) * LANES]
            outs, lses = [], []
            for sel in (head_a, jnp.logical_not(head_a)):
                qm = jnp.where(sel, qp, jnp.zeros_like(qp))
                s = lax.dot_general(qm, kp, (((1,), (1,)), ((), ())), preferred_element_type=F32)
                s = jnp.where(valid, s, NEG_INF)
                m = jnp.max(s, axis=1, keepdims=True)
                e = jnp.exp(s - m)
                den = jnp.sum(e, axis=1, keepdims=True)
                o = jnp.dot(e.astype(BF16), vp, preferred_element_type=F32)
                outs.append(o / den)
                lses.append(m + jnp.log(den))
            o_ref[0, pl.ds(t0, tq), p * LANES:(p + 1) * LANES] = jnp.where(head_a, outs[0], outs[1]).astype(o_ref.dtype)
            lse_ref[0, pl.ds(t0, tq), p * LANES:(p + 1) * LANES] = jnp.where(
                head_a, jnp.broadcast_to(lses[0], (tq, LANES)), jnp.broadcast_to(lses[1], (tq, LANES)))
        return carry

    n_tiles = length // tq
    lax.fori_loop(0, n_tiles, tile, 0, unroll=2 if n_tiles % 2 == 0 else 1)


def _attention_group(q, k, v, window, dilation):
    bsz, length, _ = q.shape
    half = window // (2 * dilation)
    tq = min(ATTN_TQ, length)
    tk = min(tq + 2 * half, length)
    spec = pl.BlockSpec((1, length, GROUP_WIDTH), lambda b, r: (b, 0, r))
    return pl.pallas_call(
        functools.partial(_attn_kernel, length=length, tq=tq, tk=tk, half=half),
        grid=(bsz, dilation),
        in_specs=[spec] * 3,
        out_specs=[spec] * 2,
        out_shape=[jax.ShapeDtypeStruct(q.shape, BF16), jax.ShapeDtypeStruct(q.shape, F32)],
        compiler_params=_params("arbitrary", "arbitrary"),
        name=f"attn_d{dilation}",
    )(q, k, v)


def _gelu_tanh(t):
    return 0.5 * t * (1.0 + jnp.tanh(0.7978845608028654 * (t + 0.044715 * t * t * t)))


def _lru_kernel(*refs, reverse, final, n_chunks):
    if final:
        (xr_ref, xp_ref, xn_ref, hf_ref, yr_ref, shift_ref, cw_ref, cb_ref, wa_ref, wx_ref, ba_ref, bx_ref, lam_ref,
         out_ref, a_s, u_s, h_s, state_s) = refs
    else:
        (xr_ref, xp_ref, xn_ref, shift_ref, cw_ref, cb_ref, wa_ref, wx_ref, ba_ref, bx_ref, lam_ref,
         out_ref, a_s, u_s, h_s, state_s) = refs
    nb, tc, ch = xr_ref.shape
    n_slabs = ch // LANES
    step = pl.program_id(1)
    chunk = (n_chunks - 1 - step) if reverse else step

    @pl.when(step == 0)
    def _():
        state_s[...] = jnp.zeros_like(state_s)

    lam = lam_ref[...]
    neg_c_softplus = -LRU_C * (jnp.maximum(-lam, 0.0) + jnp.log(1.0 + jnp.exp(-jnp.abs(lam))))
    half_log2_decay = (0.5 * LOG2_E) * neg_c_softplus
    pad_rows = jnp.zeros((shift_ref.shape[1] - tc - 2 * HALO, ch), BF16)

    def coeffs(b, carry):
        before = jnp.where(chunk > 0, xp_ref[b], jnp.zeros((HALO, ch), BF16))
        after = jnp.where(chunk < n_chunks - 1, xn_ref[b], jnp.zeros((HALO, ch), BF16))
        stack = jnp.concatenate([before, xr_ref[b], after, pad_rows], axis=0)
        row0 = pl.multiple_of(b * LRU_PITCH, SUBLANES)
        n_tiles = ch // MXU_DIM
        tile_cols = [slice(kk * MXU_DIM, (kk + 1) * MXU_DIM) for kk in range(n_tiles)]
        xcs = []
        for cols in tile_cols:
            taps = jnp.dot(shift_ref[...], stack[:, cols], preferred_element_type=F32)
            xc = cb_ref[:, cols]
            for j in range(CONV_WIDTH):
                xc = xc + cw_ref[j:j + 1, cols] * taps[j * tc:(j + 1) * tc]
            xcs.append(xc)
        pre = []
        for kk in range(n_tiles):
            xcb = xcs[kk].astype(BF16)
            pre.append((jnp.dot(xcb, wa_ref[kk], preferred_element_type=F32),
                        jnp.dot(xcb, wx_ref[kk], preferred_element_type=F32)))
        for kk, cols in enumerate(tile_cols):
            tanh_r = jnp.tanh(pre[kk][0] + ba_ref[:, cols])
            tanh_i = jnp.tanh(pre[kk][1] + bx_ref[:, cols])
            a = jnp.exp2(tanh_r * half_log2_decay[:, cols] + half_log2_decay[:, cols])
            v = 1.0 - a * a
            root = v * lax.rsqrt(jnp.maximum(v, TINY))
            half_x = 0.5 * xcs[kk]
            u = root * (tanh_i * half_x + half_x)
            for t in range(MXU_DIM // LANES):
                j = kk * (MXU_DIM // LANES) + t
                a_s[j, pl.ds(row0, tc), :] = a[:, t * LANES:(t + 1) * LANES]
                u_s[j, pl.ds(row0, tc), :] = u[:, t * LANES:(t + 1) * LANES]
        return carry

    lax.fori_loop(0, nb, coeffs, 0, unroll=2 if nb % 2 == 0 else 1)

    def scan_step(i, hs):
        t = (tc - 1 - i) if reverse else i
        new = []
        for j in range(n_slabs):
            rows = pl.ds(t, nb, stride=LRU_PITCH)
            hj = a_s[j, rows, :] * hs[j] + u_s[j, rows, :]
            h_s[j, rows, :] = hj
            new.append(hj)
        return tuple(new)

    hs = lax.fori_loop(0, tc, scan_step, tuple(state_s[j] for j in range(n_slabs)), unroll=2)
    for j in range(n_slabs):
        state_s[j] = hs[j]

    def emit(b, carry):
        row0 = pl.multiple_of(b * LRU_PITCH, SUBLANES)
        hb = jnp.concatenate([h_s[j, pl.ds(row0, tc), :] for j in range(n_slabs)], axis=1)
        if final:
            hb = (hb + hf_ref[b].astype(F32)) * _gelu_tanh(yr_ref[b].astype(F32))
        out_ref[b] = hb.astype(out_ref.dtype)
        return carry

    lax.fori_loop(0, nb, emit, 0)


def _block_diag_tiles(w):
    per_tile = MXU_DIM // w.shape[1]
    n_tiles = w.shape[0] // per_tile
    w = w.reshape(n_tiles, per_tile, w.shape[1], w.shape[2])
    eye = jnp.eye(per_tile, dtype=w.dtype)
    tiles = jnp.einsum('tpkj,pq->tpkqj', w, eye)
    return tiles.reshape(n_tiles, MXU_DIM, MXU_DIM).astype(BF16)


def _lru_pass(xr, conv_w, conv_b, w_a, b_a, w_x, b_x, lam, *, reverse, h_fwd=None, yr=None):
    bsz, s, ch = xr.shape
    nb = min(LRU_NB, bsz)
    tc = min(LRU_TC, s)
    n_chunks = s // tc
    final = h_fwd is not None
    chunk_of = (lambda c: n_chunks - 1 - c) if reverse else (lambda c: c)
    per_halo = tc // HALO
    last_halo = s // HALO - 1
    main = pl.BlockSpec((nb, tc, ch), lambda g, c: (g, chunk_of(c), 0))
    prev = pl.BlockSpec((nb, HALO, ch), lambda g, c: (g, jnp.maximum(chunk_of(c) * per_halo - 1, 0), 0))
    nxt = pl.BlockSpec((nb, HALO, ch), lambda g, c: (g, jnp.minimum((chunk_of(c) + 1) * per_halo, last_halo), 0))
    const2 = lambda g, c: (0, 0)
    const3 = lambda g, c: (0, 0, 0)
    vec = pl.BlockSpec((1, ch), const2)
    tiles = pl.BlockSpec((ch // MXU_DIM, MXU_DIM, MXU_DIM), const3)
    acts = [xr, xr, xr] + ([h_fwd, yr] if final else [])
    act_specs = [main, prev, nxt] + ([main, main] if final else [])
    slab = pltpu.VMEM((ch // LANES, nb * LRU_PITCH, LANES), F32)
    stack_rows = -(-(tc + 2 * HALO) // MXU_DIM) * MXU_DIM
    tap_row = HALO - CONV_WIDTH // 2 + jnp.arange(CONV_WIDTH)[:, None] + jnp.arange(tc)[None, :]
    shifts = (tap_row.reshape(-1, 1) == jnp.arange(stack_rows)[None, :]).astype(BF16)
    return pl.pallas_call(
        functools.partial(_lru_kernel, reverse=reverse, final=final, n_chunks=n_chunks),
        grid=(bsz // nb, n_chunks),
        in_specs=act_specs + [pl.BlockSpec(shifts.shape, const2), pl.BlockSpec((CONV_WIDTH, ch), const2), vec,
                              tiles, tiles, vec, vec, vec],
        out_specs=main,
        out_shape=jax.ShapeDtypeStruct((bsz, s, ch), BF16),
        scratch_shapes=[slab, slab, slab, pltpu.VMEM((ch // LANES, nb, LANES), F32)],
        compiler_params=_params("arbitrary", "arbitrary"),
        name="lru_bwd" if reverse else "lru_fwd",
    )(*acts, shifts, conv_w.reshape(CONV_WIDTH, ch), conv_b.reshape(1, ch), _block_diag_tiles(0.5 * w_a),
      _block_diag_tiles(0.5 * w_x), (0.5 * b_a).reshape(1, ch), (0.5 * b_x).reshape(1, ch), lam.reshape(1, ch))


def _load_classes(in_ref, row0, n, relayout_s, dilation):
    if dilation == 1:
        return in_ref[0, row0:row0 + n, :].astype(F32)
    n_tiles = GROUP_WIDTH // LANES
    rows = n // dilation
    for r in range(dilation):
        for t in range(n_tiles):
            lo = r * GROUP_WIDTH + t * LANES
            relayout_s[t, pl.ds(r, rows, stride=dilation), :] = (
                in_ref[0, row0 // dilation:row0 // dilation + rows, lo:lo + LANES].astype(F32))
    return jnp.concatenate([relayout_s[t] for t in range(n_tiles)], axis=1)


def _merge_kernel(o0_ref, o1_ref, o2_ref, l0_ref, l1_ref, l2_ref, lru_ref, ga_ref, gl_ref, x_ref, gate_ref,
                  wab_ref, wlb_ref, wout_ref, nw_ref, shift_ref, scale_ref, wr_ref,
                  x1_ref, h2_ref, aff_ref, so1_s, so2_s, sl1_s, sl2_s):
    dil = [dilation for _, dilation in ATTN_GROUPS]
    sub = x_ref.shape[1] // SUB_TILES
    for part in range(SUB_TILES):
        row0 = part * sub
        rows = slice(row0, row0 + sub)
        l0 = _load_classes(l0_ref, row0, sub, None, dil[0])
        l1 = _load_classes(l1_ref, row0, sub, sl1_s.at[part], dil[1])
        l2 = _load_classes(l2_ref, row0, sub, sl2_s.at[part], dil[2])
        o0 = _load_classes(o0_ref, row0, sub, None, dil[0])
        o1 = _load_classes(o1_ref, row0, sub, so1_s.at[part], dil[1])
        o2 = _load_classes(o2_ref, row0, sub, so2_s.at[part], dil[2])
        m = jnp.maximum(jnp.maximum(l0, l1), l2)
        e0, e1, e2 = jnp.exp(l0 - m), jnp.exp(l1 - m), jnp.exp(l2 - m)
        attn = (e0 * o0 + e1 * o1 + e2 * o2) / (e0 + e1 + e2)
        branch_a = jnp.dot(attn.astype(BF16), wab_ref[...], preferred_element_type=F32)
        branch_l = jnp.dot(lru_ref[0, rows, :], wlb_ref[...], preferred_element_type=F32)
        merged = (_sigmoid(ga_ref[0, rows, :].astype(F32)) * branch_a
                  + _sigmoid(gl_ref[0, rows, :].astype(F32)) * branch_l)
        mix = jnp.dot(merged.astype(BF16), wout_ref[...], preferred_element_type=F32)
        x1 = x_ref[0, rows, :] + gate_ref[0] * mix
        x1_ref[0, rows, :] = x1
        ms = jnp.mean(x1 * x1, axis=-1, keepdims=True)
        h2 = x1 * lax.rsqrt(ms + NORM_EPS) * nw_ref[...]
        h2 = h2 * (1.0 + scale_ref[0]) + shift_ref[0]
        h2_ref[0, rows, :] = h2.astype(h2_ref.dtype)
        logits = lax.dot_general(wr_ref[...], h2, (((1,), (1,)), ((), ())), preferred_element_type=F32,
                                 precision=lax.Precision.HIGHEST)
        z = jnp.exp(logits - jnp.max(logits, axis=0, keepdims=True))
        aff_ref[0, :, rows] = z / jnp.sum(z, axis=0, keepdims=True)


def _merge(outs, lses, lru, ga, gl, x, gate1, w_ab, w_lb, w_out, norm_w, shift2, scale2, w_router):
    bsz, s, d = x.shape
    tm = min(ROW_TILE, s)
    n_exp = w_router.shape[1]
    row = lambda b, i: (b, i, 0)
    per_batch = lambda b, i: (b, 0, 0)
    const = lambda b, i: (0, 0)
    grp = [pl.BlockSpec((1, tm // dil, dil * GROUP_WIDTH), row) for _, dil in ATTN_GROUPS]
    wide = pl.BlockSpec((1, tm, d), row)
    mod = pl.BlockSpec((1, 1, d), per_batch)
    full = lambda a: pl.BlockSpec(a.shape, const)
    w_ab, w_lb, w_out = w_ab.astype(BF16), w_lb.astype(BF16), w_out.astype(BF16)
    w_rt = w_router.T
    return pl.pallas_call(
        _merge_kernel,
        grid=(bsz, s // tm),
        in_specs=grp * 2 + [wide] * 4 + [mod, full(w_ab), full(w_lb), full(w_out),
                                          pl.BlockSpec((1, d), const), mod, mod, full(w_rt)],
        out_specs=[wide, wide, pl.BlockSpec((1, n_exp, tm), lambda b, i: (b, 0, i))],
        out_shape=[jax.ShapeDtypeStruct((bsz, s, d), F32), jax.ShapeDtypeStruct((bsz, s, d), BF16),
                   jax.ShapeDtypeStruct((bsz, n_exp, s), F32)],
        scratch_shapes=[pltpu.VMEM((SUB_TILES, GROUP_WIDTH // LANES, tm // SUB_TILES, LANES), F32)] * 4,
        compiler_params=_params("arbitrary", "arbitrary"),
        name="merge",
    )(*outs, *lses, lru, ga, gl, x, gate1.reshape(bsz, 1, d), w_ab, w_lb, w_out, norm_w.reshape(1, d),
      shift2.reshape(bsz, 1, d), scale2.reshape(bsz, 1, d), w_rt)


def _prefix_counts(flags, strict_upper):
    n_exp, s = flags.shape
    n_tiles = s // MXU_DIM
    stacked = jnp.concatenate([flags[:, k * MXU_DIM:(k + 1) * MXU_DIM] for k in range(n_tiles)], axis=0)
    within = jnp.dot(stacked.astype(BF16), strict_upper, preferred_element_type=F32)
    totals = jnp.sum(stacked, axis=1, keepdims=True)
    run = jnp.zeros((n_exp, 1), F32)
    pieces, bases = [], []
    for k in range(n_tiles):
        bases.append(run)
        pieces.append(within[k * n_exp:(k + 1) * n_exp] + run)
        run = run + totals[k * n_exp:(k + 1) * n_exp]
    bases.append(run)
    return jnp.concatenate(pieces, axis=1), bases


def _route_kernel(aff_ref, upper_ref, slot_ref, base_ref, *, capacity):
    aff = aff_ref[0]
    cap = jnp.float32(capacity)

    def refine(i, thr_bits):
        cand = thr_bits | jnp.left_shift(jnp.int32(1), 30 - i)
        cnt = jnp.sum(jnp.where(aff >= pltpu.bitcast(cand, F32), 1.0, 0.0), axis=1, keepdims=True)
        return jnp.where(cnt >= cap, cand, thr_bits)

    thr_bits = lax.fori_loop(0, 31, refine, jnp.zeros((aff.shape[0], 1), jnp.int32))
    thr = pltpu.bitcast(thr_bits, F32)
    above = jnp.where(aff > thr, 1.0, 0.0)
    tied = jnp.where(aff == thr, 1.0, 0.0)
    need = cap - jnp.sum(above, axis=1, keepdims=True)
    upper = upper_ref[...]
    tie_rank, _ = _prefix_counts(tied, upper)
    chosen = above + tied * jnp.where(tie_rank < need, 1.0, 0.0)
    slot, bases = _prefix_counts(chosen, upper)
    slot_ref[0] = jnp.where(chosen > 0.0, slot, -1.0).astype(jnp.int32)
    lane = lax.broadcasted_iota(jnp.int32, (aff.shape[0], LANES), 1)
    table = jnp.zeros((aff.shape[0], LANES), F32)
    for k, bk in enumerate(bases):
        table = jnp.where(lane == k, bk, table)
    base_ref[0] = table.astype(jnp.int32)


def _route(aff, capacity):
    bsz, n_exp, s = aff.shape
    idx = jnp.arange(MXU_DIM)
    upper = (idx[:, None] < idx[None, :]).astype(BF16)
    per_batch = lambda b: (b, 0, 0)
    return pl.pallas_call(
        functools.partial(_route_kernel, capacity=capacity),
        grid=(bsz,),
        in_specs=[pl.BlockSpec((1, n_exp, s), per_batch), pl.BlockSpec((MXU_DIM, MXU_DIM), lambda b: (0, 0))],
        out_specs=[pl.BlockSpec((1, n_exp, s), per_batch), pl.BlockSpec((1, n_exp, LANES), per_batch)],
        out_shape=[jax.ShapeDtypeStruct((bsz, n_exp, s), jnp.int32),
                   jax.ShapeDtypeStruct((bsz, n_exp, LANES), jnp.int32)],
        compiler_params=_params("arbitrary"),
        name="route",
    )(aff, upper)


def _window_plan(base_ref, row, chunk, capacity):
    first = base_ref[row + chunk]
    end = base_ref[row + chunk + 1]
    start = jnp.minimum((first // ROUTE_ALIGN) * ROUTE_ALIGN, capacity - ROUTE_WINDOW)
    n_windows = (end - start + ROUTE_WINDOW - 1) // ROUTE_WINDOW
    return pl.multiple_of(start, ROUTE_ALIGN), n_windows


def _later_window(start, k, capacity):
    lo = start + k * ROUTE_WINDOW
    return lo, pl.multiple_of(jnp.minimum(lo, capacity - ROUTE_WINDOW), ROUTE_ALIGN)


def _dispatch_kernel(base_ref, h_ref, slot_ref, aff_ref, xe_ref, gs_ref, *, n_exp, n_chunks, capacity):
    b, eg, c = pl.program_id(0), pl.program_id(1), pl.program_id(2)
    group = xe_ref.shape[0]
    chunk = h_ref.shape[1]

    @pl.when(c == 0)
    def _():
        xe_ref[...] = jnp.zeros_like(xe_ref)
        gs_ref[...] = jnp.zeros_like(gs_ref)

    h = h_ref[0]
    w_iota = lax.broadcasted_iota(jnp.int32, (ROUTE_WINDOW, chunk), 0)
    plans, hots = [], []
    for e in range(group):
        row = (b * n_exp + eg * group + e) * (n_chunks + 1)
        start, n_windows = _window_plan(base_ref, row, c, capacity)
        plans.append((start, n_windows))
        hots.append((slot_ref[0, e:e + 1, :] - start) == w_iota)
    stack = jnp.concatenate([jnp.where(hot, 1.0, 0.0) for hot in hots], axis=0).astype(BF16)
    rows = jnp.dot(stack, h, preferred_element_type=F32)
    for e in range(group):
        start, n_windows = plans[e]
        win = pl.ds(start, ROUTE_WINDOW)
        xe_ref[e, 0, win, :] += rows[e * ROUTE_WINDOW:(e + 1) * ROUTE_WINDOW].astype(xe_ref.dtype)
        gs_ref[e, 0, win, :] += jnp.sum(jnp.where(hots[e], aff_ref[0, e:e + 1, :], 0.0), axis=1, keepdims=True)

        def more(k, carry, e=e, start=start):
            lo, st = _later_window(start, k, capacity)
            slots = slot_ref[0, e:e + 1, :]
            hot = jnp.logical_and(slots - st == w_iota, slots >= lo)
            extra = jnp.dot(jnp.where(hot, 1.0, 0.0).astype(BF16), h, preferred_element_type=F32)
            xe_ref[e, 0, pl.ds(st, ROUTE_WINDOW), :] += extra.astype(xe_ref.dtype)
            gs_ref[e, 0, pl.ds(st, ROUTE_WINDOW), :] += jnp.sum(
                jnp.where(hot, aff_ref[0, e:e + 1, :], 0.0), axis=1, keepdims=True)
            return carry

        lax.fori_loop(1, n_windows, more, 0)


def _dispatch(h2, slot, aff, base_flat, capacity):
    bsz, s, d = h2.shape
    n_exp = slot.shape[1]
    chunk = min(ROUTE_CHUNK, s)
    group = min(EXPERT_GROUP, n_exp)
    n_chunks = s // chunk
    return pl.pallas_call(
        functools.partial(_dispatch_kernel, n_exp=n_exp, n_chunks=n_chunks, capacity=capacity),
        grid_spec=pltpu.PrefetchScalarGridSpec(
            num_scalar_prefetch=1,
            grid=(bsz, n_exp // group, n_chunks),
            in_specs=[pl.BlockSpec((1, chunk, d), lambda b, g, c, base: (b, c, 0)),
                      pl.BlockSpec((1, group, chunk), lambda b, g, c, base: (b, g, c)),
                      pl.BlockSpec((1, group, chunk), lambda b, g, c, base: (b, g, c))],
            out_specs=[pl.BlockSpec((group, 1, capacity, d), lambda b, g, c, base: (g, b, 0, 0)),
                       pl.BlockSpec((group, 1, capacity, 1), lambda b, g, c, base: (g, b, 0, 0))]),
        out_shape=[jax.ShapeDtypeStruct((n_exp, bsz, capacity, d), BF16),
                   jax.ShapeDtypeStruct((n_exp, bsz, capacity, 1), F32)],
        compiler_params=_params("arbitrary", "arbitrary", "arbitrary"),
        name="dispatch",
    )(base_flat, h2, slot, aff)


def _expert_kernel(x_ref, g_ref, wg_ref, wu_ref, wd_ref, y_ref):
    x = x_ref[0, 0]
    gate = jnp.dot(x, wg_ref[0], preferred_element_type=F32)
    up = jnp.dot(x, wu_ref[0], preferred_element_type=F32)
    he = (gate * _sigmoid(gate) * up).astype(BF16)
    y_ref[0, 0] = (jnp.dot(he, wd_ref[0], preferred_element_type=F32) * g_ref[0, 0]).astype(y_ref.dtype)


def _experts(xe, gates, w_gate, w_up, w_down):
    n_exp, bsz, cap, d = xe.shape
    ff = w_gate.shape[2]
    row = lambda e, i: (e, i, 0, 0)
    per_expert = lambda e, i: (e, 0, 0)
    return pl.pallas_call(
        _expert_kernel,
        grid=(n_exp, bsz),
        in_specs=[pl.BlockSpec((1, 1, cap, d), row), pl.BlockSpec((1, 1, cap, 1), row),
                  pl.BlockSpec((1, d, ff), per_expert), pl.BlockSpec((1, d, ff), per_expert),
                  pl.BlockSpec((1, ff, d), per_expert)],
        out_specs=pl.BlockSpec((1, 1, cap, d), row),
        out_shape=jax.ShapeDtypeStruct(xe.shape, BF16),
        compiler_params=_params("arbitrary", "arbitrary"),
        name="experts",
    )(xe, gates, w_gate.astype(BF16), w_up.astype(BF16), w_down.astype(BF16))


def _combine_kernel(base_ref, y_ref, slot_ref, x1_ref, gate_ref, out_ref, acc_s, *, n_chunks, capacity):
    b, c = pl.program_id(0), pl.program_id(1)
    n_exp = y_ref.shape[0]
    chunk = x1_ref.shape[1]
    w_iota = lax.broadcasted_iota(jnp.int32, (ROUTE_WINDOW, chunk), 0)
    plans, hots, wins = [], [], []
    for e in range(n_exp):
        start, n_windows = _window_plan(base_ref, (b * n_exp + e) * (n_chunks + 1), c, capacity)
        plans.append((start, n_windows))
        hots.append(jnp.where((slot_ref[0, e:e + 1, :] - start) == w_iota, 1.0, 0.0))
        wins.append(y_ref[e, 0, pl.ds(start, ROUTE_WINDOW), :])
    hot = jnp.concatenate(hots, axis=0).astype(BF16)
    ywin = jnp.concatenate(wins, axis=0)
    acc_s[...] = lax.dot_general(hot, ywin, (((0,), (0,)), ((), ())), preferred_element_type=F32)
    for e in range(n_exp):
        start, n_windows = plans[e]

        def more(k, carry, e=e, start=start):
            lo, st = _later_window(start, k, capacity)
            slots = slot_ref[0, e:e + 1, :]
            sel = jnp.logical_and(slots - st == w_iota, slots >= lo)
            acc_s[...] += lax.dot_general(jnp.where(sel, 1.0, 0.0).astype(BF16),
                                          y_ref[e, 0, pl.ds(st, ROUTE_WINDOW), :],
                                          (((0,), (0,)), ((), ())), preferred_element_type=F32)
            return carry

        lax.fori_loop(1, n_windows, more, 0)
    out_ref[0] = x1_ref[0] + gate_ref[0] * acc_s[...]


def _combine(ys, slot, base_flat, x1, gate2, capacity):
    bsz, s, d = x1.shape
    n_exp = ys.shape[0]
    chunk = min(ROUTE_CHUNK, s)
    n_chunks = s // chunk
    return pl.pallas_call(
        functools.partial(_combine_kernel, n_chunks=n_chunks, capacity=capacity),
        grid_spec=pltpu.PrefetchScalarGridSpec(
            num_scalar_prefetch=1,
            grid=(bsz, n_chunks),
            in_specs=[pl.BlockSpec((n_exp, 1, capacity, d), lambda b, c, base: (0, b, 0, 0),
                                   pipeline_mode=pl.Buffered(1)),
                      pl.BlockSpec((1, n_exp, chunk), lambda b, c, base: (b, 0, c)),
                      pl.BlockSpec((1, chunk, d), lambda b, c, base: (b, c, 0)),
                      pl.BlockSpec((1, 1, d), lambda b, c, base: (b, 0, 0))],
            out_specs=pl.BlockSpec((1, chunk, d), lambda b, c, base: (b, c, 0)),
            scratch_shapes=[pltpu.VMEM((chunk, d), F32)]),
        out_shape=jax.ShapeDtypeStruct(x1.shape, F32),
        compiler_params=_params("arbitrary", "arbitrary"),
        name="combine",
    )(base_flat, ys, slot, x1, gate2.reshape(bsz, 1, d))


def _layer(x, c, positions, w_ada, b_ada, norm1_w, w_in, q_norm_w, k_norm_w, conv_w, conv_b, lru_w_a, lru_b_a,
           lru_w_x, lru_b_x, lru_lambda, w_attn_branch, w_lru_branch, w_out, norm2_w, w_router, w_gate, w_up,
           w_down):
    bsz, s, d = x.shape
    mod = _ada(c, w_ada, b_ada)
    shift1, scale1, gate1, shift2, scale2, gate2 = jnp.split(mod, 6, axis=-1)

    (q0, q1, q2, k0, k1, k2, v0, v1, v2, xr, yr, ga, gl) = _inproj(x, positions, shift1, scale1, norm1_w, w_in,
                                                                  q_norm_w, k_norm_w)

    outs, lses = [], []
    for (window, dilation), q, k, v in zip(ATTN_GROUPS, (q0, q1, q2), (k0, k1, k2), (v0, v1, v2)):
        o, lse = _attention_group(q, k, v, window, dilation)
        outs.append(o)
        lses.append(lse)

    h_fwd = _lru_pass(xr, conv_w, conv_b, lru_w_a[0], lru_b_a[0], lru_w_x[0], lru_b_x[0], lru_lambda[0],
                      reverse=False)
    lru = _lru_pass(xr, conv_w, conv_b, lru_w_a[1], lru_b_a[1], lru_w_x[1], lru_b_x[1], lru_lambda[1],
                    reverse=True, h_fwd=h_fwd, yr=yr)

    x1, h2, aff = _merge(outs, lses, lru, ga, gl, x, gate1, w_attn_branch, w_lru_branch, w_out, norm2_w,
                         shift2, scale2, w_router)

    capacity = max(1, CAPACITY_FACTOR * s // N_EXPERTS)
    n_chunks = s // min(ROUTE_CHUNK, s)
    slot, base_table = _route(aff, capacity)
    base_flat = base_table[:, :, :n_chunks + 1].reshape(-1)
    xe, gates = _dispatch(h2, slot, aff, base_flat, capacity)
    ys = _experts(xe, gates, w_gate, w_up, w_down)
    return _combine(ys, slot, base_flat, x1, gate2, capacity)


def kernel(x, c, positions, w_ada, b_ada, norm1_w, w_in, q_norm_w, k_norm_w, conv_w, conv_b, lru_w_a, lru_b_a,
           lru_w_x, lru_b_x, lru_lambda, w_attn_branch, w_lru_branch, w_out, norm2_w, w_router, w_gate, w_up,
           w_down):
    for l in range(w_ada.shape[0]):
        x = _layer(x, c, positions, w_ada[l], b_ada[l], norm1_w[l], w_in[l], q_norm_w[l], k_norm_w[l],
                   conv_w[l, :, 0, :], conv_b[l], lru_w_a[l], lru_b_a[l], lru_w_x[l], lru_b_x[l], lru_lambda[l],
                   w_attn_branch[l], w_lru_branch[l], w_out[l], norm2_w[l], w_router[l], w_gate[l], w_up[l],
                   w_down[l])
    return x
```

```python
import functools

import jax
import jax.numpy as jnp
from jax import lax
from jax.experimental import pallas as pl
from jax.experimental.pallas import tpu as pltpu

F32 = jnp.float32
BF16 = jnp.bfloat16

HEAD_DIM = 64
HEADS_PER_GROUP = 4
GROUP_WIDTH = HEADS_PER_GROUP * HEAD_DIM
ATTN_GROUPS = ((128, 1), (512, 4), (2048, 16))
ATTN_WIDTH = GROUP_WIDTH * len(ATTN_GROUPS)
LRU_BLOCKS = 16
LRU_C = 8.0
CONV_WIDTH = 4
N_EXPERTS = 16
CAPACITY_FACTOR = 2
ROPE_THETA = 10000.0
NORM_EPS = 1e-6
NEG_INF = -1e30
LOG2_E = 1.4426950408889634
TINY = 1e-30

LANES = 128
SUBLANES = 8
MXU_DIM = 256
VMEM_LIMIT = 56 * 1024 * 1024

ROW_TILE = 512
SUB_TILES = 1
ATTN_TQ = 128
LRU_TC = 128
LRU_NB = 8
LRU_PITCH = LRU_TC + 8
HALO = 16
ROUTE_CHUNK = 256
ROUTE_WINDOW = 64
ROUTE_ALIGN = 16
EXPERT_GROUP = 8


def _sigmoid(t):
    return 0.5 * jnp.tanh(0.5 * t) + 0.5


def _params(*sem):
    return pltpu.CompilerParams(dimension_semantics=sem, vmem_limit_bytes=VMEM_LIMIT)


def _ada_kernel(c_ref, w_ref, b_ref, o_ref):
    c = c_ref[...]
    o_ref[...] = jnp.dot(c * _sigmoid(c), w_ref[...], preferred_element_type=F32,
                         precision=lax.Precision.HIGHEST) + b_ref[...]


def _ada(c, w, b):
    bsz, d = c.shape
    n = w.shape[1]
    tn = n // 4
    return pl.pallas_call(
        _ada_kernel,
        grid=(n // tn,),
        in_specs=[pl.BlockSpec((bsz, d), lambda j: (0, 0)),
                  pl.BlockSpec((d, tn), lambda j: (0, j)),
                  pl.BlockSpec((1, tn), lambda j: (0, j))],
        out_specs=pl.BlockSpec((bsz, tn), lambda j: (0, j)),
        out_shape=jax.ShapeDtypeStruct((bsz, n), F32),
        compiler_params=_params("arbitrary"),
        name="ada",
    )(c, w, b.reshape(1, n))


def _store_classes(val, out_ref, row0, relayout_s, dilation):
    n = val.shape[0]
    if dilation == 1:
        out_ref[0, row0:row0 + n, :] = val.astype(out_ref.dtype)
        return
    n_tiles = GROUP_WIDTH // LANES
    rows = n // dilation
    for t in range(n_tiles):
        relayout_s[t] = val[:, t * LANES:(t + 1) * LANES]
    for r in range(dilation):
        for t in range(n_tiles):
            lo = r * GROUP_WIDTH + t * LANES
            out_ref[0, row0 // dilation:row0 // dilation + rows, lo:lo + LANES] = (
                relayout_s[t, pl.ds(r, rows, stride=dilation), :].astype(out_ref.dtype))


def _inproj_kernel(x_ref, pos_ref, shift_ref, scale_ref, nw_ref, w_ref, qnw_ref, knw_ref, invf_ref, seg_ref,
                   q0_ref, q1_ref, q2_ref, k0_ref, k1_ref, k2_ref, v0_ref, v1_ref, v2_ref,
                   xr_ref, yr_ref, ga_ref, gl_ref, relayout_s):
    tm, d = x_ref.shape[1:]
    sub = tm // SUB_TILES
    lane = lax.broadcasted_iota(jnp.int32, (sub, GROUP_WIDTH), 1)
    first_half = (lane & (HEAD_DIM // 2)) == 0
    seg = seg_ref[...]

    for part in range(SUB_TILES):
        row0 = part * sub
        x = x_ref[0, row0:row0 + sub, :]
        ms = jnp.mean(x * x, axis=-1, keepdims=True)
        h = x * lax.rsqrt(ms + NORM_EPS) * nw_ref[...]
        hb = (h * (1.0 + scale_ref[0]) + shift_ref[0]).astype(BF16)

        ang = pos_ref[0, row0:row0 + sub, :].astype(F32) * invf_ref[...]
        cos = jnp.cos(ang)
        sin = jnp.sin(ang)
        cos2 = jnp.concatenate([cos, cos], axis=1)
        sin2 = jnp.concatenate([sin, sin], axis=1)
        sin2 = jnp.where(first_half, -sin2, sin2)
        scratch = relayout_s.at[part]

        def norm_rope(off, w_norm_ref, out_ref, j):
            lo_c = j * GROUP_WIDTH
            t = jnp.dot(hb, w_ref[:, off + lo_c:off + lo_c + GROUP_WIDTH], preferred_element_type=F32)
            msq = jnp.dot((t * t).astype(BF16), seg, preferred_element_type=F32) * (1.0 / HEAD_DIM)
            y = t * lax.rsqrt(msq + NORM_EPS) * w_norm_ref[:, lo_c:lo_c + GROUP_WIDTH]
            rot = jnp.where(first_half, pltpu.roll(y, GROUP_WIDTH - HEAD_DIM // 2, 1),
                            pltpu.roll(y, HEAD_DIM // 2, 1))
            _store_classes(y * cos2 + rot * sin2, out_ref, row0, scratch, ATTN_GROUPS[j][1])

        def value(out_ref, j):
            off = 2 * ATTN_WIDTH + j * GROUP_WIDTH
            v = jnp.dot(hb, w_ref[:, off:off + GROUP_WIDTH], preferred_element_type=F32)
            _store_classes(v, out_ref, row0, scratch, ATTN_GROUPS[j][1])

        def wide(out_ref, i):
            off = 3 * ATTN_WIDTH + i * d
            out_ref[0, row0:row0 + sub, :] = jnp.dot(hb, w_ref[:, off:off + d],
                                                     preferred_element_type=F32).astype(out_ref.dtype)

        wide(xr_ref, 0)
        wide(yr_ref, 1)
        norm_rope(0, qnw_ref, q0_ref, 0)
        wide(ga_ref, 2)
        norm_rope(ATTN_WIDTH, knw_ref, k0_ref, 0)
        wide(gl_ref, 3)
        norm_rope(0, qnw_ref, q1_ref, 1)
        value(v0_ref, 0)
        norm_rope(ATTN_WIDTH, knw_ref, k1_ref, 1)
        value(v1_ref, 1)
        norm_rope(0, qnw_ref, q2_ref, 2)
        value(v2_ref, 2)
        norm_rope(ATTN_WIDTH, knw_ref, k2_ref, 2)


def _inproj(x, positions, shift, scale, norm_w, w_in, q_norm_w, k_norm_w):
    bsz, s, d = x.shape
    tm = min(ROW_TILE, s)
    half = HEAD_DIM // 2
    inv_freq = ROPE_THETA ** (-jnp.arange(half, dtype=F32) / half)
    inv_freq = jnp.tile(inv_freq, LANES // half).reshape(1, LANES)
    head_of = jnp.arange(GROUP_WIDTH) // HEAD_DIM
    seg = (head_of[:, None] == head_of[None, :]).astype(BF16)
    row = lambda b, i: (b, i, 0)
    per_batch = lambda b, i: (b, 0, 0)
    const = lambda b, i: (0, 0)
    wide = [jax.ShapeDtypeStruct((bsz, s, d), BF16)] * 4
    narrow = [jax.ShapeDtypeStruct((bsz, s // dil, dil * GROUP_WIDTH), BF16) for _, dil in ATTN_GROUPS] * 3
    narrow_specs = [pl.BlockSpec((1, tm // dil, dil * GROUP_WIDTH), row) for _, dil in ATTN_GROUPS] * 3
    return pl.pallas_call(
        _inproj_kernel,
        grid=(bsz, s // tm),
        in_specs=[pl.BlockSpec((1, tm, d), row),
                  pl.BlockSpec((1, tm, 1), row),
                  pl.BlockSpec((1, 1, d), per_batch),
                  pl.BlockSpec((1, 1, d), per_batch),
                  pl.BlockSpec((1, d), const),
                  pl.BlockSpec(w_in.shape, const),
                  pl.BlockSpec((1, ATTN_WIDTH), const),
                  pl.BlockSpec((1, ATTN_WIDTH), const),
                  pl.BlockSpec((1, LANES), const),
                  pl.BlockSpec((GROUP_WIDTH, GROUP_WIDTH), const)],
        out_specs=narrow_specs + [pl.BlockSpec((1, tm, d), row)] * 4,
        out_shape=narrow + wide,
        scratch_shapes=[pltpu.VMEM((SUB_TILES, GROUP_WIDTH // LANES, tm // SUB_TILES, LANES), F32)],
        compiler_params=_params("arbitrary", "arbitrary"),
        name="inproj",
    )(x, positions.reshape(bsz, s, 1), shift.reshape(bsz, 1, d), scale.reshape(bsz, 1, d),
      norm_w.reshape(1, d), w_in.astype(BF16), (q_norm_w * HEAD_DIM ** -0.5).reshape(1, ATTN_WIDTH),
      k_norm_w.reshape(1, ATTN_WIDTH), inv_freq, seg)


def _attn_kernel(q_ref, k_ref, v_ref, o_ref, lse_ref, *, length, tq, tk, half):
    rel0 = (lax.broadcasted_iota(jnp.int32, (tq, tk), 1) - lax.broadcasted_iota(jnp.int32, (tq, tk), 0))
    lane = lax.broadcasted_iota(jnp.int32, (tq, LANES), 1)
    head_a = lane < HEAD_DIM

    def tile(i, carry):
        t0 = pl.multiple_of(i * tq, tq)
        ws = pl.multiple_of(jnp.clip(t0 - half, 0, length - tk), half)
        valid = jnp.abs(rel0 + (ws - t0)) <= half
        q = q_ref[0, pl.ds(t0, tq), :]
        k = k_ref[0, pl.ds(ws, tk), :]
        v = v_ref[0, pl.ds(ws, tk), :]
        for p in range(GROUP_WIDTH // LANES):
            qp = q[:, p * LANES:(p + 1) * LANES]
            kp = k[:, p * LANES:(p + 1) * LANES]
            vp = v[:, p * LANES:(p + 1) * LANES]
            outs, lses = [], []
            for sel in (head_a, jnp.logical_not(head_a)):
                qm = jnp.where(sel, qp, jnp.zeros_like(qp))
                s = lax.dot_general(qm, kp, (((1,), (1,)), ((), ())), preferred_element_type=F32)
                s = jnp.where(valid, s, NEG_INF)
                m = jnp.max(s, axis=1, keepdims=True)
                e = jnp.exp(s - m)
                den = jnp.sum(e, axis=1, keepdims=True)
                o = jnp.dot(e.astype(BF16), vp, preferred_element_type=F32)
                outs.append(o / den)
                lses.append(m + jnp.log(den))
            o_ref[0, pl.ds(t0, tq), p * LANES:(p + 1) * LANES] = jnp.where(head_a, outs[0], outs[1]).astype(o_ref.dtype)
            lse_ref[0, pl.ds(t0, tq), p * LANES:(p + 1) * LANES] = jnp.where(
                head_a, jnp.broadcast_to(lses[0], (tq, LANES)), jnp.broadcast_to(lses[1], (tq, LANES)))
        return carry

    n_tiles = length // tq
    lax.fori_loop(0, n_tiles, tile, 0, unroll=2 if n_tiles % 2 == 0 else 1)


def _attention_group(q, k, v, window, dilation):
    bsz, length, _ = q.shape
    half = window // (2 * dilation)
    tq = min(ATTN_TQ, length)
    tk = min(tq + 2 * half, length)
    spec = pl.BlockSpec((1, length, GROUP_WIDTH), lambda b, r: (b, 0, r))
    return pl.pallas_call(
        functools.partial(_attn_kernel, length=length, tq=tq, tk=tk, half=half),
        grid=(bsz, dilation),
        in_specs=[spec] * 3,
        out_specs=[spec] * 2,
        out_shape=[jax.ShapeDtypeStruct(q.shape, BF16), jax.ShapeDtypeStruct(q.shape, F32)],
        compiler_params=_params("arbitrary", "arbitrary"),
        name=f"attn_d{dilation}",
    )(q, k, v)


def _gelu_tanh(t):
    return 0.5 * t * (1.0 + jnp.tanh(0.7978845608028654 * (t + 0.044715 * t * t * t)))


def _lru_kernel(*refs, reverse, final, n_chunks):
    if final:
        (xr_ref, xp_ref, xn_ref, hf_ref, yr_ref, shift_ref, cw_ref, cb_ref, wa_ref, wx_ref, ba_ref, bx_ref, lam_ref,
         out_ref, a_s, u_s, h_s, state_s) = refs
    else:
        (xr_ref, xp_ref, xn_ref, shift_ref, cw_ref, cb_ref, wa_ref, wx_ref, ba_ref, bx_ref, lam_ref,
         out_ref, a_s, u_s, h_s, state_s) = refs
    nb, tc, ch = xr_ref.shape
    n_slabs = ch // LANES
    step = pl.program_id(1)
    chunk = (n_chunks - 1 - step) if reverse else step

    @pl.when(step == 0)
    def _():
        state_s[...] = jnp.zeros_like(state_s)

    lam = lam_ref[...]
    neg_c_softplus = -LRU_C * (jnp.maximum(-lam, 0.0) + jnp.log(1.0 + jnp.exp(-jnp.abs(lam))))
    half_log2_decay = (0.5 * LOG2_E) * neg_c_softplus
    pad_rows = jnp.zeros((shift_ref.shape[1] - tc - 2 * HALO, ch), BF16)

    def coeffs(b, carry):
        before = jnp.where(chunk > 0, xp_ref[b], jnp.zeros((HALO, ch), BF16))
        after = jnp.where(chunk < n_chunks - 1, xn_ref[b], jnp.zeros((HALO, ch), BF16))
        stack = jnp.concatenate([before, xr_ref[b], after, pad_rows], axis=0)
        row0 = pl.multiple_of(b * LRU_PITCH, SUBLANES)
        n_tiles = ch // MXU_DIM
        tile_cols = [slice(kk * MXU_DIM, (kk + 1) * MXU_DIM) for kk in range(n_tiles)]
        xcs = []
        for cols in tile_cols:
            taps = jnp.dot(shift_ref[...], stack[:, cols], preferred_element_type=F32)
            xc = cb_ref[:, cols]
            for j in range(CONV_WIDTH):
                xc = xc + cw_ref[j:j + 1, cols] * taps[j * tc:(j + 1) * tc]
            xcs.append(xc)
        pre = []
        for kk in range(n_tiles):
            xcb = xcs[kk].astype(BF16)
            pre.append((jnp.dot(xcb, wa_ref[kk], preferred_element_type=F32),
                        jnp.dot(xcb, wx_ref[kk], preferred_element_type=F32)))
        for kk, cols in enumerate(tile_cols):
            tanh_r = jnp.tanh(pre[kk][0] + ba_ref[:, cols])
            tanh_i = jnp.tanh(pre[kk][1] + bx_ref[:, cols])
            a = jnp.exp2(tanh_r * half_log2_decay[:, cols] + half_log2_decay[:, cols])
            v = 1.0 - a * a
            root = v * lax.rsqrt(jnp.maximum(v, TINY))
            half_x = 0.5 * xcs[kk]
            u = root * (tanh_i * half_x + half_x)
            for t in range(MXU_DIM // LANES):
                j = kk * (MXU_DIM // LANES) + t
                a_s[j, pl.ds(row0, tc), :] = a[:, t * LANES:(t + 1) * LANES]
                u_s[j, pl.ds(row0, tc), :] = u[:, t * LANES:(t + 1) * LANES]
        return carry

    lax.fori_loop(0, nb, coeffs, 0, unroll=2 if nb % 2 == 0 else 1)

    def scan_step(i, hs):
        t = (tc - 1 - i) if reverse else i
        new = []
        for j in range(n_slabs):
            rows = pl.ds(t, nb, stride=LRU_PITCH)
            hj = a_s[j, rows, :] * hs[j] + u_s[j, rows, :]
            h_s[j, rows, :] = hj
            new.append(hj)
        return tuple(new)

    hs = lax.fori_loop(0, tc, scan_step, tuple(state_s[j] for j in range(n_slabs)), unroll=2)
    for j in range(n_slabs):
        state_s[j] = hs[j]

    def emit(b, carry):
        row0 = pl.multiple_of(b * LRU_PITCH, SUBLANES)
        hb = jnp.concatenate([h_s[j, pl.ds(row0, tc), :] for j in range(n_slabs)], axis=1)
        if final:
            hb = (hb + hf_ref[b].astype(F32)) * _gelu_tanh(yr_ref[b].astype(F32))
        out_ref[b] = hb.astype(out_ref.dtype)
        return carry

    lax.fori_loop(0, nb, emit, 0)


def _block_diag_tiles(w):
    per_tile = MXU_DIM // w.shape[1]
    n_tiles = w.shape[0] // per_tile
    w = w.reshape(n_tiles, per_tile, w.shape[1], w.shape[2])
    eye = jnp.eye(per_tile, dtype=w.dtype)
    tiles = jnp.einsum('tpkj,pq->tpkqj', w, eye)
    return tiles.reshape(n_tiles, MXU_DIM, MXU_DIM).astype(BF16)


def _lru_pass(xr, conv_w, conv_b, w_a, b_a, w_x, b_x, lam, *, reverse, h_fwd=None, yr=None):
    bsz, s, ch = xr.shape
    nb = min(LRU_NB, bsz)
    tc = min(LRU_TC, s)
    n_chunks = s // tc
    final = h_fwd is not None
    chunk_of = (lambda c: n_chunks - 1 - c) if reverse else (lambda c: c)
    per_halo = tc // HALO
    last_halo = s // HALO - 1
    main = pl.BlockSpec((nb, tc, ch), lambda g, c: (g, chunk_of(c), 0))
    prev = pl.BlockSpec((nb, HALO, ch), lambda g, c: (g, jnp.maximum(chunk_of(c) * per_halo - 1, 0), 0))
    nxt = pl.BlockSpec((nb, HALO, ch), lambda g, c: (g, jnp.minimum((chunk_of(c) + 1) * per_halo, last_halo), 0))
    const2 = lambda g, c: (0, 0)
    const3 = lambda g, c: (0, 0, 0)
    vec = pl.BlockSpec((1, ch), const2)
    tiles = pl.BlockSpec((ch // MXU_DIM, MXU_DIM, MXU_DIM), const3)
    acts = [xr, xr, xr] + ([h_fwd, yr] if final else [])
    act_specs = [main, prev, nxt] + ([main, main] if final else [])
    slab = pltpu.VMEM((ch // LANES, nb * LRU_PITCH, LANES), F32)
    stack_rows = -(-(tc + 2 * HALO) // MXU_DIM) * MXU_DIM
    tap_row = HALO - CONV_WIDTH // 2 + jnp.arange(CONV_WIDTH)[:, None] + jnp.arange(tc)[None, :]
    shifts = (tap_row.reshape(-1, 1) == jnp.arange(stack_rows)[None, :]).astype(BF16)
    return pl.pallas_call(
        functools.partial(_lru_kernel, reverse=reverse, final=final, n_chunks=n_chunks),
        grid=(bsz // nb, n_chunks),
        in_specs=act_specs + [pl.BlockSpec(shifts.shape, const2), pl.BlockSpec((CONV_WIDTH, ch), const2), vec,
                              tiles, tiles, vec, vec, vec],
        out_specs=main,
        out_shape=jax.ShapeDtypeStruct((bsz, s, ch), BF16),
        scratch_shapes=[slab, slab, slab, pltpu.VMEM((ch // LANES, nb, LANES), F32)],
        compiler_params=_params("arbitrary", "arbitrary"),
        name="lru_bwd" if reverse else "lru_fwd",
    )(*acts, shifts, conv_w.reshape(CONV_WIDTH, ch), conv_b.reshape(1, ch), _block_diag_tiles(0.5 * w_a),
      _block_diag_tiles(0.5 * w_x), (0.5 * b_a).reshape(1, ch), (0.5 * b_x).reshape(1, ch), lam.reshape(1, ch))


def _load_classes(in_ref, row0, n, relayout_s, dilation):
    if dilation == 1:
        return in_ref[0, row0:row0 + n, :].astype(F32)
    n_tiles = GROUP_WIDTH // LANES
    rows = n // dilation
    for r in range(dilation):
        for t in range(n_tiles):
            lo = r * GROUP_WIDTH + t * LANES
            relayout_s[t, pl.ds(r, rows, stride=dilation), :] = (
                in_ref[0, row0 // dilation:row0 // dilation + rows, lo:lo + LANES].astype(F32))
    return jnp.concatenate([relayout_s[t] for t in range(n_tiles)], axis=1)


def _merge_kernel(o0_ref, o1_ref, o2_ref, l0_ref, l1_ref, l2_ref, lru_ref, ga_ref, gl_ref, x_ref, gate_ref,
                  wab_ref, wlb_ref, wout_ref, nw_ref, shift_ref, scale_ref, wr_ref,
                  x1_ref, h2_ref, aff_ref, so1_s, so2_s, sl1_s, sl2_s):
    dil = [dilation for _, dilation in ATTN_GROUPS]
    sub = x_ref.shape[1] // SUB_TILES
    for part in range(SUB_TILES):
        row0 = part * sub
        rows = slice(row0, row0 + sub)
        l0 = _load_classes(l0_ref, row0, sub, None, dil[0])
        l1 = _load_classes(l1_ref, row0, sub, sl1_s.at[part], dil[1])
        l2 = _load_classes(l2_ref, row0, sub, sl2_s.at[part], dil[2])
        o0 = _load_classes(o0_ref, row0, sub, None, dil[0])
        o1 = _load_classes(o1_ref, row0, sub, so1_s.at[part], dil[1])
        o2 = _load_classes(o2_ref, row0, sub, so2_s.at[part], dil[2])
        m = jnp.maximum(jnp.maximum(l0, l1), l2)
        e0, e1, e2 = jnp.exp(l0 - m), jnp.exp(l1 - m), jnp.exp(l2 - m)
        attn = (e0 * o0 + e1 * o1 + e2 * o2) / (e0 + e1 + e2)
        branch_a = jnp.dot(attn.astype(BF16), wab_ref[...], preferred_element_type=F32)
        branch_l = jnp.dot(lru_ref[0, rows, :], wlb_ref[...], preferred_element_type=F32)
        merged = (_sigmoid(ga_ref[0, rows, :].astype(F32)) * branch_a
                  + _sigmoid(gl_ref[0, rows, :].astype(F32)) * branch_l)
        mix = jnp.dot(merged.astype(BF16), wout_ref[...], preferred_element_type=F32)
        x1 = x_ref[0, rows, :] + gate_ref[0] * mix
        x1_ref[0, rows, :] = x1
        ms = jnp.mean(x1 * x1, axis=-1, keepdims=True)
        h2 = x1 * lax.rsqrt(ms + NORM_EPS) * nw_ref[...]
        h2 = h2 * (1.0 + scale_ref[0]) + shift_ref[0]
        h2_ref[0, rows, :] = h2.astype(h2_ref.dtype)
        logits = lax.dot_general(wr_ref[...], h2, (((1,), (1,)), ((), ())), preferred_element_type=F32,
                                 precision=lax.Precision.HIGHEST)
        z = jnp.exp(logits - jnp.max(logits, axis=0, keepdims=True))
        aff_ref[0, :, rows] = z / jnp.sum(z, axis=0, keepdims=True)


def _merge(outs, lses, lru, ga, gl, x, gate1, w_ab, w_lb, w_out, norm_w, shift2, scale2, w_router):
    bsz, s, d = x.shape
    tm = min(ROW_TILE, s)
    n_exp = w_router.shape[1]
    row = lambda b, i: (b, i, 0)
    per_batch = lambda b, i: (b, 0, 0)
    const = lambda b, i: (0, 0)
    grp = [pl.BlockSpec((1, tm // dil, dil * GROUP_WIDTH), row) for _, dil in ATTN_GROUPS]
    wide = pl.BlockSpec((1, tm, d), row)
    mod = pl.BlockSpec((1, 1, d), per_batch)
    full = lambda a: pl.BlockSpec(a.shape, const)
    w_ab, w_lb, w_out = w_ab.astype(BF16), w_lb.astype(BF16), w_out.astype(BF16)
    w_rt = w_router.T
    return pl.pallas_call(
        _merge_kernel,
        grid=(bsz, s // tm),
        in_specs=grp * 2 + [wide] * 4 + [mod, full(w_ab), full(w_lb), full(w_out),
                                          pl.BlockSpec((1, d), const), mod, mod, full(w_rt)],
        out_specs=[wide, wide, pl.BlockSpec((1, n_exp, tm), lambda b, i: (b, 0, i))],
        out_shape=[jax.ShapeDtypeStruct((bsz, s, d), F32), jax.ShapeDtypeStruct((bsz, s, d), BF16),
                   jax.ShapeDtypeStruct((bsz, n_exp, s), F32)],
        scratch_shapes=[pltpu.VMEM((SUB_TILES, GROUP_WIDTH // LANES, tm // SUB_TILES, LANES), F32)] * 4,
        compiler_params=_params("arbitrary", "arbitrary"),
        name="merge",
    )(*outs, *lses, lru, ga, gl, x, gate1.reshape(bsz, 1, d), w_ab, w_lb, w_out, norm_w.reshape(1, d),
      shift2.reshape(bsz, 1, d), scale2.reshape(bsz, 1, d), w_rt)


def _prefix_counts(flags, strict_upper):
    n_exp, s = flags.shape
    n_tiles = s // MXU_DIM
    stacked = jnp.concatenate([flags[:, k * MXU_DIM:(k + 1) * MXU_DIM] for k in range(n_tiles)], axis=0)
    within = jnp.dot(stacked.astype(BF16), strict_upper, preferred_element_type=F32)
    totals = jnp.sum(stacked, axis=1, keepdims=True)
    run = jnp.zeros((n_exp, 1), F32)
    pieces, bases = [], []
    for k in range(n_tiles):
        bases.append(run)
        pieces.append(within[k * n_exp:(k + 1) * n_exp] + run)
        run = run + totals[k * n_exp:(k + 1) * n_exp]
    bases.append(run)
    return jnp.concatenate(pieces, axis=1), bases


def _route_kernel(aff_ref, upper_ref, slot_ref, base_ref, *, capacity):
    aff = aff_ref[0]
    cap = jnp.float32(capacity)

    def refine(i, thr_bits):
        cand = thr_bits | jnp.left_shift(jnp.int32(1), 30 - i)
        cnt = jnp.sum(jnp.where(aff >= pltpu.bitcast(cand, F32), 1.0, 0.0), axis=1, keepdims=True)
        return jnp.where(cnt >= cap, cand, thr_bits)

    thr_bits = lax.fori_loop(0, 31, refine, jnp.zeros((aff.shape[0], 1), jnp.int32))
    thr = pltpu.bitcast(thr_bits, F32)
    above = jnp.where(aff > thr, 1.0, 0.0)
    tied = jnp.where(aff == thr, 1.0, 0.0)
    need = cap - jnp.sum(above, axis=1, keepdims=True)
    upper = upper_ref[...]
    tie_rank, _ = _prefix_counts(tied, upper)
    chosen = above + tied * jnp.where(tie_rank < need, 1.0, 0.0)
    slot, bases = _prefix_counts(chosen, upper)
    slot_ref[0] = jnp.where(chosen > 0.0, slot, -1.0).astype(jnp.int32)
    lane = lax.broadcasted_iota(jnp.int32, (aff.shape[0], LANES), 1)
    table = jnp.zeros((aff.shape[0], LANES), F32)
    for k, bk in enumerate(bases):
        table = jnp.where(lane == k, bk, table)
    base_ref[0] = table.astype(jnp.int32)


def _route(aff, capacity):
    bsz, n_exp, s = aff.shape
    idx = jnp.arange(MXU_DIM)
    upper = (idx[:, None] < idx[None, :]).astype(BF16)
    per_batch = lambda b: (b, 0, 0)
    return pl.pallas_call(
        functools.partial(_route_kernel, capacity=capacity),
        grid=(bsz,),
        in_specs=[pl.BlockSpec((1, n_exp, s), per_batch), pl.BlockSpec((MXU_DIM, MXU_DIM), lambda b: (0, 0))],
        out_specs=[pl.BlockSpec((1, n_exp, s), per_batch), pl.BlockSpec((1, n_exp, LANES), per_batch)],
        out_shape=[jax.ShapeDtypeStruct((bsz, n_exp, s), jnp.int32),
                   jax.ShapeDtypeStruct((bsz, n_exp, LANES), jnp.int32)],
        compiler_params=_params("arbitrary"),
        name="route",
    )(aff, upper)


def _window_plan(base_ref, row, chunk, capacity):
    first = base_ref[row + chunk]
    end = base_ref[row + chunk + 1]
    start = jnp.minimum((first // ROUTE_ALIGN) * ROUTE_ALIGN, capacity - ROUTE_WINDOW)
    n_windows = (end - start + ROUTE_WINDOW - 1) // ROUTE_WINDOW
    return pl.multiple_of(start, ROUTE_ALIGN), n_windows


def _later_window(start, k, capacity):
    lo = start + k * ROUTE_WINDOW
    return lo, pl.multiple_of(jnp.minimum(lo, capacity - ROUTE_WINDOW), ROUTE_ALIGN)


def _dispatch_kernel(base_ref, h_ref, slot_ref, aff_ref, xe_ref, gs_ref, *, n_exp, n_chunks, capacity):
    b, eg, c = pl.program_id(0), pl.program_id(1), pl.program_id(2)
    group = xe_ref.shape[0]
    chunk = h_ref.shape[1]

    @pl.when(c == 0)
    def _():
        xe_ref[...] = jnp.zeros_like(xe_ref)
        gs_ref[...] = jnp.zeros_like(gs_ref)

    h = h_ref[0]
    w_iota = lax.broadcasted_iota(jnp.int32, (ROUTE_WINDOW, chunk), 0)
    plans, hots = [], []
    for e in range(group):
        row = (b * n_exp + eg * group + e) * (n_chunks + 1)
        start, n_windows = _window_plan(base_ref, row, c, capacity)
        plans.append((start, n_windows))
        hots.append((slot_ref[0, e:e + 1, :] - start) == w_iota)
    stack = jnp.concatenate([jnp.where(hot, 1.0, 0.0) for hot in hots], axis=0).astype(BF16)
    rows = jnp.dot(stack, h, preferred_element_type=F32)
    most_windows = plans[0][1]
    for e in range(group):
        start, n_windows = plans[e]
        most_windows = jnp.maximum(most_windows, n_windows)
        win = pl.ds(start, ROUTE_WINDOW)
        xe_ref[e, 0, win, :] += rows[e * ROUTE_WINDOW:(e + 1) * ROUTE_WINDOW].astype(xe_ref.dtype)
        gs_ref[e, 0, win, :] += jnp.sum(jnp.where(hots[e], aff_ref[0, e:e + 1, :], 0.0), axis=1, keepdims=True)

    @pl.when(most_windows > 1)
    def _():
        for e in range(group):
            start, n_windows = plans[e]

            def more(k, carry, e=e, start=start):
                lo, st = _later_window(start, k, capacity)
                slots = slot_ref[0, e:e + 1, :]
                hot = jnp.logical_and(slots - st == w_iota, slots >= lo)
                extra = jnp.dot(jnp.where(hot, 1.0, 0.0).astype(BF16), h, preferred_element_type=F32)
                xe_ref[e, 0, pl.ds(st, ROUTE_WINDOW), :] += extra.astype(xe_ref.dtype)
                gs_ref[e, 0, pl.ds(st, ROUTE_WINDOW), :] += jnp.sum(
                    jnp.where(hot, aff_ref[0, e:e + 1, :], 0.0), axis=1, keepdims=True)
                return carry

            lax.fori_loop(1, n_windows, more, 0)


def _dispatch(h2, slot, aff, base_flat, capacity):
    bsz, s, d = h2.shape
    n_exp = slot.shape[1]
    chunk = min(ROUTE_CHUNK, s)
    group = min(EXPERT_GROUP, n_exp)
    n_chunks = s // chunk
    return pl.pallas_call(
        functools.partial(_dispatch_kernel, n_exp=n_exp, n_chunks=n_chunks, capacity=capacity),
        grid_spec=pltpu.PrefetchScalarGridSpec(
            num_scalar_prefetch=1,
            grid=(bsz, n_exp // group, n_chunks),
            in_specs=[pl.BlockSpec((1, chunk, d), lambda b, g, c, base: (b, c, 0)),
                      pl.BlockSpec((1, group, chunk), lambda b, g, c, base: (b, g, c)),
                      pl.BlockSpec((1, group, chunk), lambda b, g, c, base: (b, g, c))],
            out_specs=[pl.BlockSpec((group, 1, capacity, d), lambda b, g, c, base: (g, b, 0, 0)),
                       pl.BlockSpec((group, 1, capacity, 1), lambda b, g, c, base: (g, b, 0, 0))]),
        out_shape=[jax.ShapeDtypeStruct((n_exp, bsz, capacity, d), BF16),
                   jax.ShapeDtypeStruct((n_exp, bsz, capacity, 1), F32)],
        compiler_params=_params("arbitrary", "arbitrary", "arbitrary"),
        name="dispatch",
    )(base_flat, h2, slot, aff)


def _expert_kernel(x_ref, g_ref, wg_ref, wu_ref, wd_ref, y_ref):
    x = x_ref[0, 0]
    gate = jnp.dot(x, wg_ref[0], preferred_element_type=F32)
    up = jnp.dot(x, wu_ref[0], preferred_element_type=F32)
    he = (gate * _sigmoid(gate) * up).astype(BF16)
    y_ref[0, 0] = (jnp.dot(he, wd_ref[0], preferred_element_type=F32) * g_ref[0, 0]).astype(y_ref.dtype)


def _experts(xe, gates, w_gate, w_up, w_down):
    n_exp, bsz, cap, d = xe.shape
    ff = w_gate.shape[2]
    row = lambda e, i: (e, i, 0, 0)
    per_expert = lambda e, i: (e, 0, 0)
    return pl.pallas_call(
        _expert_kernel,
        grid=(n_exp, bsz),
        in_specs=[pl.BlockSpec((1, 1, cap, d), row), pl.BlockSpec((1, 1, cap, 1), row),
                  pl.BlockSpec((1, d, ff), per_expert), pl.BlockSpec((1, d, ff), per_expert),
                  pl.BlockSpec((1, ff, d), per_expert)],
        out_specs=pl.BlockSpec((1, 1, cap, d), row),
        out_shape=jax.ShapeDtypeStruct(xe.shape, BF16),
        compiler_params=_params("arbitrary", "arbitrary"),
        name="experts",
    )(xe, gates, w_gate.astype(BF16), w_up.astype(BF16), w_down.astype(BF16))


def _combine_kernel(base_ref, y_ref, slot_ref, x1_ref, gate_ref, out_ref, acc_s, *, n_chunks, capacity):
    b, c = pl.program_id(0), pl.program_id(1)
    n_exp = y_ref.shape[0]
    chunk = x1_ref.shape[1]
    w_iota = lax.broadcasted_iota(jnp.int32, (ROUTE_WINDOW, chunk), 0)
    plans, hots, wins = [], [], []
    for e in range(n_exp):
        start, n_windows = _window_plan(base_ref, (b * n_exp + e) * (n_chunks + 1), c, capacity)
        plans.append((start, n_windows))
        hots.append(jnp.where((slot_ref[0, e:e + 1, :] - start) == w_iota, 1.0, 0.0))
        wins.append(y_ref[e, 0, pl.ds(start, ROUTE_WINDOW), :])
    hot = jnp.concatenate(hots, axis=0).astype(BF16)
    ywin = jnp.concatenate(wins, axis=0)
    acc = lax.dot_general(hot, ywin, (((0,), (0,)), ((), ())), preferred_element_type=F32)
    most_windows = plans[0][1]
    for _, n_windows in plans[1:]:
        most_windows = jnp.maximum(most_windows, n_windows)

    @pl.when(most_windows <= 1)
    def _():
        out_ref[0] = x1_ref[0] + gate_ref[0] * acc

    @pl.when(most_windows > 1)
    def _():
        acc_s[...] = acc
        for e in range(n_exp):
            start, n_windows = plans[e]

            def more(k, carry, e=e, start=start):
                lo, st = _later_window(start, k, capacity)
                slots = slot_ref[0, e:e + 1, :]
                sel = jnp.logical_and(slots - st == w_iota, slots >= lo)
                acc_s[...] += lax.dot_general(jnp.where(sel, 1.0, 0.0).astype(BF16),
                                              y_ref[e, 0, pl.ds(st, ROUTE_WINDOW), :],
                                              (((0,), (0,)), ((), ())), preferred_element_type=F32)
                return carry

            lax.fori_loop(1, n_windows, more, 0)
        out_ref[0] = x1_ref[0] + gate_ref[0] * acc_s[...]


def _combine(ys, slot, base_flat, x1, gate2, capacity):
    bsz, s, d = x1.shape
    n_exp = ys.shape[0]
    chunk = min(ROUTE_CHUNK, s)
    n_chunks = s // chunk
    return pl.pallas_call(
        functools.partial(_combine_kernel, n_chunks=n_chunks, capacity=capacity),
        grid_spec=pltpu.PrefetchScalarGridSpec(
            num_scalar_prefetch=1,
            grid=(bsz, n_chunks),
            in_specs=[pl.BlockSpec((n_exp, 1, capacity, d), lambda b, c, base: (0, b, 0, 0),
                                   pipeline_mode=pl.Buffered(1)),
                      pl.BlockSpec((1, n_exp, chunk), lambda b, c, base: (b, 0, c)),
                      pl.BlockSpec((1, chunk, d), lambda b, c, base: (b, c, 0)),
                      pl.BlockSpec((1, 1, d), lambda b, c, base: (b, 0, 0))],
            out_specs=pl.BlockSpec((1, chunk, d), lambda b, c, base: (b, c, 0)),
            scratch_shapes=[pltpu.VMEM((chunk, d), F32)]),
        out_shape=jax.ShapeDtypeStruct(x1.shape, F32),
        compiler_params=_params("arbitrary", "arbitrary"),
        name="combine",
    )(base_flat, ys, slot, x1, gate2.reshape(bsz, 1, d))


def _layer(x, c, positions, w_ada, b_ada, norm1_w, w_in, q_norm_w, k_norm_w, conv_w, conv_b, lru_w_a, lru_b_a,
           lru_w_x, lru_b_x, lru_lambda, w_attn_branch, w_lru_branch, w_out, norm2_w, w_router, w_gate, w_up,
           w_down):
    bsz, s, d = x.shape
    mod = _ada(c, w_ada, b_ada)
    shift1, scale1, gate1, shift2, scale2, gate2 = jnp.split(mod, 6, axis=-1)

    (q0, q1, q2, k0, k1, k2, v0, v1, v2, xr, yr, ga, gl) = _inproj(x, positions, shift1, scale1, norm1_w, w_in,
                                                                  q_norm_w, k_norm_w)

    outs, lses = [], []
    for (window, dilation), q, k, v in zip(ATTN_GROUPS, (q0, q1, q2), (k0, k1, k2), (v0, v1, v2)):
        o, lse = _attention_group(q, k, v, window, dilation)
        outs.append(o)
        lses.append(lse)

    h_fwd = _lru_pass(xr, conv_w, conv_b, lru_w_a[0], lru_b_a[0], lru_w_x[0], lru_b_x[0], lru_lambda[0],
                      reverse=False)
    lru = _lru_pass(xr, conv_w, conv_b, lru_w_a[1], lru_b_a[1], lru_w_x[1], lru_b_x[1], lru_lambda[1],
                    reverse=True, h_fwd=h_fwd, yr=yr)

    x1, h2, aff = _merge(outs, lses, lru, ga, gl, x, gate1, w_attn_branch, w_lru_branch, w_out, norm2_w,
                         shift2, scale2, w_router)

    capacity = max(1, CAPACITY_FACTOR * s // N_EXPERTS)
    n_chunks = s // min(ROUTE_CHUNK, s)
    slot, base_table = _route(aff, capacity)
    base_flat = base_table[:, :, :n_chunks + 1].reshape(-1)
    xe, gates = _dispatch(h2, slot, aff, base_flat, capacity)
    ys = _experts(xe, gates, w_gate, w_up, w_down)
    return _combine(ys, slot, base_flat, x1, gate2, capacity)


def kernel(x, c, positions, w_ada, b_ada, norm1_w, w_in, q_norm_w, k_norm_w, conv_w, conv_b, lru_w_a, lru_b_a,
           lru_w_x, lru_b_x, lru_lambda, w_attn_branch, w_lru_branch, w_out, norm2_w, w_router, w_gate, w_up,
           w_down):
    for l in range(w_ada.shape[0]):
        x = _layer(x, c, positions, w_ada[l], b_ada[l], norm1_w[l], w_in[l], q_norm_w[l], k_norm_w[l],
                   conv_w[l, :, 0, :], conv_b[l], lru_w_a[l], lru_b_a[l], lru_w_x[l], lru_b_x[l], lru_lambda[l],
                   w_attn_branch[l], w_lru_branch[l], w_out[l], norm2_w[l], w_router[l], w_gate[l], w_up[l],
                   w_down[l])
    return x
```

```python
import functools

import jax
import jax.numpy as jnp
from jax import lax
from jax.experimental import pallas as pl
from jax.experimental.pallas import tpu as pltpu

F32 = jnp.float32
BF16 = jnp.bfloat16

HEAD_DIM = 64
HEADS_PER_GROUP = 4
GROUP_WIDTH = HEADS_PER_GROUP * HEAD_DIM
ATTN_GROUPS = ((128, 1), (512, 4), (2048, 16))
ATTN_WIDTH = GROUP_WIDTH * len(ATTN_GROUPS)
LRU_BLOCKS = 16
LRU_C = 8.0
CONV_WIDTH = 4
N_EXPERTS = 16
CAPACITY_FACTOR = 2
ROPE_THETA = 10000.0
NORM_EPS = 1e-6
NEG_INF = -1e30
LOG2_E = 1.4426950408889634
TINY = 1e-30

LANES = 128
SUBLANES = 8
MXU_DIM = 256
VMEM_LIMIT = 56 * 1024 * 1024

ROW_TILE = 512
SUB_TILES = 1
ATTN_TQ = 128
LRU_TC = 128
LRU_NB = 8
HALO = 16
ROUTE_CHUNK = 256
ROUTE_WINDOW = 64
ROUTE_ALIGN = 16
EXPERT_GROUP = 16
ROPE_PACK = LANES // (HEAD_DIM // 2)


def _sigmoid(t):
    return 0.5 * jnp.tanh(0.5 * t) + 0.5


def _params(*sem):
    return pltpu.CompilerParams(dimension_semantics=sem, vmem_limit_bytes=VMEM_LIMIT)


def _ada_kernel(c_ref, w_ref, b_ref, o_ref):
    c = c_ref[...]
    o_ref[...] = jnp.dot(c * _sigmoid(c), w_ref[...], preferred_element_type=F32,
                         precision=lax.Precision.HIGHEST) + b_ref[...]


def _ada(c, w, b):
    bsz, d = c.shape
    n = w.shape[1]
    tn = n // 4
    return pl.pallas_call(
        _ada_kernel,
        grid=(n // tn,),
        in_specs=[pl.BlockSpec((bsz, d), lambda j: (0, 0)),
                  pl.BlockSpec((d, tn), lambda j: (0, j)),
                  pl.BlockSpec((1, tn), lambda j: (0, j))],
        out_specs=pl.BlockSpec((bsz, tn), lambda j: (0, j)),
        out_shape=jax.ShapeDtypeStruct((bsz, n), F32),
        compiler_params=_params("arbitrary"),
        name="ada",
    )(c, w, b.reshape(1, n))


def _store_classes(val, out_ref, row0, relayout_s, dilation):
    n = val.shape[0]
    if dilation == 1:
        out_ref[0, row0:row0 + n, :] = val.astype(out_ref.dtype)
        return
    n_tiles = GROUP_WIDTH // LANES
    rows = n // dilation
    for t in range(n_tiles):
        relayout_s[t] = val[:, t * LANES:(t + 1) * LANES]
    for r in range(dilation):
        for t in range(n_tiles):
            lo = r * GROUP_WIDTH + t * LANES
            out_ref[0, row0 // dilation:row0 // dilation + rows, lo:lo + LANES] = (
                relayout_s[t, pl.ds(r, rows, stride=dilation), :].astype(out_ref.dtype))


def _inproj_kernel(x_ref, pos_ref, shift_ref, scale_ref, nw_ref, w_ref, qnw_ref, knw_ref, invf_ref, seg_ref,
                   spread_ref, q0_ref, q1_ref, q2_ref, k0_ref, k1_ref, k2_ref, v0_ref, v1_ref, v2_ref,
                   xr_ref, yr_ref, ga_ref, gl_ref, relayout_s, cos_s, sin_s):
    tm, d = x_ref.shape[1:]
    sub = tm // SUB_TILES
    lane = lax.broadcasted_iota(jnp.int32, (sub, GROUP_WIDTH), 1)
    first_half = (lane & (HEAD_DIM // 2)) == 0
    seg = seg_ref[...]

    for part in range(SUB_TILES):
        row0 = part * sub
        x = x_ref[0, row0:row0 + sub, :]
        ms = jnp.mean(x * x, axis=-1, keepdims=True)
        h = x * lax.rsqrt(ms + NORM_EPS) * nw_ref[...]
        hb = (h * (1.0 + scale_ref[0]) + shift_ref[0]).astype(BF16)

        packed_rows = sub // ROPE_PACK
        ang = pos_ref[0, row0 // ROPE_PACK:row0 // ROPE_PACK + packed_rows, :].astype(F32) * invf_ref[...]
        cos_p = jnp.cos(ang)
        sin_p = jnp.sin(ang)
        for p in range(ROPE_PACK):
            rows_p = pl.ds(p, packed_rows, stride=ROPE_PACK)
            cos_s[rows_p, :] = jnp.dot(cos_p, spread_ref[p], preferred_element_type=F32,
                                       precision=lax.Precision.HIGHEST)
            sin_s[rows_p, :] = jnp.dot(sin_p, spread_ref[p], preferred_element_type=F32,
                                       precision=lax.Precision.HIGHEST)
        cos = cos_s[...]
        sin = sin_s[...]
        cos2 = jnp.concatenate([cos, cos], axis=1)
        sin2 = jnp.concatenate([sin, sin], axis=1)
        sin2 = jnp.where(first_half, -sin2, sin2)
        scratch = relayout_s.at[part]

        def norm_rope(off, w_norm_ref, out_ref, j):
            lo_c = j * GROUP_WIDTH
            t = jnp.dot(hb, w_ref[:, off + lo_c:off + lo_c + GROUP_WIDTH], preferred_element_type=F32)
            msq = jnp.dot((t * t).astype(BF16), seg, preferred_element_type=F32) * (1.0 / HEAD_DIM)
            y = t * lax.rsqrt(msq + NORM_EPS) * w_norm_ref[:, lo_c:lo_c + GROUP_WIDTH]
            rot = jnp.where(first_half, pltpu.roll(y, GROUP_WIDTH - HEAD_DIM // 2, 1),
                            pltpu.roll(y, HEAD_DIM // 2, 1))
            _store_classes(y * cos2 + rot * sin2, out_ref, row0, scratch, ATTN_GROUPS[j][1])

        def value(out_ref, j):
            off = 2 * ATTN_WIDTH + j * GROUP_WIDTH
            v = jnp.dot(hb, w_ref[:, off:off + GROUP_WIDTH], preferred_element_type=F32)
            _store_classes(v, out_ref, row0, scratch, ATTN_GROUPS[j][1])

        def wide(out_ref, i):
            off = 3 * ATTN_WIDTH + i * d
            out_ref[0, row0:row0 + sub, :] = jnp.dot(hb, w_ref[:, off:off + d],
                                                     preferred_element_type=F32).astype(out_ref.dtype)

        wide(xr_ref, 0)
        wide(yr_ref, 1)
        norm_rope(0, qnw_ref, q0_ref, 0)
        wide(ga_ref, 2)
        norm_rope(ATTN_WIDTH, knw_ref, k0_ref, 0)
        wide(gl_ref, 3)
        norm_rope(0, qnw_ref, q1_ref, 1)
        value(v0_ref, 0)
        norm_rope(ATTN_WIDTH, knw_ref, k1_ref, 1)
        value(v1_ref, 1)
        norm_rope(0, qnw_ref, q2_ref, 2)
        value(v2_ref, 2)
        norm_rope(ATTN_WIDTH, knw_ref, k2_ref, 2)


def _inproj(x, positions, shift, scale, norm_w, w_in, q_norm_w, k_norm_w):
    bsz, s, d = x.shape
    tm = min(ROW_TILE, s)
    half = HEAD_DIM // 2
    inv_freq = ROPE_THETA ** (-jnp.arange(half, dtype=F32) / half)
    inv_freq = jnp.tile(inv_freq, ROPE_PACK).reshape(1, LANES)
    pos_packed = jnp.repeat(positions.reshape(bsz, s // ROPE_PACK, ROPE_PACK), half, axis=-1)
    lane = jnp.arange(LANES)
    spread = (lane[None, :, None] == (jnp.arange(ROPE_PACK)[:, None, None] * half + lane[None, None, :] % half))
    spread = spread.astype(F32)
    head_of = jnp.arange(GROUP_WIDTH) // HEAD_DIM
    seg = (head_of[:, None] == head_of[None, :]).astype(BF16)
    row = lambda b, i: (b, i, 0)
    per_batch = lambda b, i: (b, 0, 0)
    const = lambda b, i: (0, 0)
    wide = [jax.ShapeDtypeStruct((bsz, s, d), BF16)] * 4
    narrow = [jax.ShapeDtypeStruct((bsz, s // dil, dil * GROUP_WIDTH), BF16) for _, dil in ATTN_GROUPS] * 3
    narrow_specs = [pl.BlockSpec((1, tm // dil, dil * GROUP_WIDTH), row) for _, dil in ATTN_GROUPS] * 3
    return pl.pallas_call(
        _inproj_kernel,
        grid=(bsz, s // tm),
        in_specs=[pl.BlockSpec((1, tm, d), row),
                  pl.BlockSpec((1, tm // ROPE_PACK, LANES), row),
                  pl.BlockSpec((1, 1, d), per_batch),
                  pl.BlockSpec((1, 1, d), per_batch),
                  pl.BlockSpec((1, d), const),
                  pl.BlockSpec(w_in.shape, const),
                  pl.BlockSpec((1, ATTN_WIDTH), const),
                  pl.BlockSpec((1, ATTN_WIDTH), const),
                  pl.BlockSpec((1, LANES), const),
                  pl.BlockSpec((GROUP_WIDTH, GROUP_WIDTH), const),
                  pl.BlockSpec(spread.shape, lambda b, i: (0, 0, 0))],
        out_specs=narrow_specs + [pl.BlockSpec((1, tm, d), row)] * 4,
        out_shape=narrow + wide,
        scratch_shapes=[pltpu.VMEM((SUB_TILES, GROUP_WIDTH // LANES, tm // SUB_TILES, LANES), F32),
                        pltpu.VMEM((tm // SUB_TILES, LANES), F32), pltpu.VMEM((tm // SUB_TILES, LANES), F32)],
        compiler_params=_params("arbitrary", "arbitrary"),
        name="inproj",
    )(x, pos_packed, shift.reshape(bsz, 1, d), scale.reshape(bsz, 1, d),
      norm_w.reshape(1, d), w_in.astype(BF16), (q_norm_w * HEAD_DIM ** -0.5).reshape(1, ATTN_WIDTH),
      k_norm_w.reshape(1, ATTN_WIDTH), inv_freq, seg, spread)


def _attn_kernel(q_ref, k_ref, v_ref, o_ref, lse_ref, *, length, tq, tk, half):
    rel0 = (lax.broadcasted_iota(jnp.int32, (2 * tq, tk), 1)
            - (lax.broadcasted_iota(jnp.int32, (2 * tq, tk), 0) & (tq - 1)))
    lane = lax.broadcasted_iota(jnp.int32, (tq, LANES), 1)
    head_a = lane < HEAD_DIM

    def tile(i, carry):
        t0 = pl.multiple_of(i * tq, tq)
        ws = pl.multiple_of(jnp.clip(t0 - half, 0, length - tk), half)
        valid = jnp.abs(rel0 + (ws - t0)) <= half
        q = q_ref[0, pl.ds(t0, tq), :]
        k = k_ref[0, pl.ds(ws, tk), :]
        v = v_ref[0, pl.ds(ws, tk), :]
        for p in range(GROUP_WIDTH // LANES):
            qp = q[:, p * LANES:(p + 1) * LANES]
            kp = k[:, p * LANES:(p + 1) * LANES]
            vp = v[:, p * LANES:(p + 1) * LANES]
            zero = jnp.zeros_like(qp)
            q2 = jnp.concatenate([jnp.where(head_a, qp, zero), jnp.where(head_a, zero, qp)], axis=0)
            s = lax.dot_general(q2, kp, (((1,), (1,)), ((), ())), preferred_element_type=F32)
            s = jnp.where(valid, s, NEG_INF)
            m = jnp.max(s, axis=1, keepdims=True)
            e = jnp.exp(s - m)
            den = jnp.sum(e, axis=1, keepdims=True)
            o = jnp.dot(e.astype(BF16), vp, preferred_element_type=F32) * (1.0 / den)
            lse = jnp.broadcast_to(m + jnp.log(den), (2 * tq, LANES))
            o_ref[0, pl.ds(t0, tq), p * LANES:(p + 1) * LANES] = jnp.where(head_a, o[:tq], o[tq:]).astype(o_ref.dtype)
            lse_ref[0, pl.ds(t0, tq), p * LANES:(p + 1) * LANES] = jnp.where(head_a, lse[:tq], lse[tq:])
        return carry

    n_tiles = length // tq
    lax.fori_loop(0, n_tiles, tile, 0, unroll=2 if n_tiles % 2 == 0 else 1)


def _attention_group(q, k, v, window, dilation):
    bsz, length, _ = q.shape
    half = window // (2 * dilation)
    tq = min(ATTN_TQ, length)
    tk = min(tq + 2 * half, length)
    spec = pl.BlockSpec((1, length, GROUP_WIDTH), lambda b, r: (b, 0, r))
    return pl.pallas_call(
        functools.partial(_attn_kernel, length=length, tq=tq, tk=tk, half=half),
        grid=(bsz, dilation),
        in_specs=[spec] * 3,
        out_specs=[spec] * 2,
        out_shape=[jax.ShapeDtypeStruct(q.shape, BF16), jax.ShapeDtypeStruct(q.shape, F32)],
        compiler_params=_params("arbitrary", "arbitrary"),
        name=f"attn_d{dilation}",
    )(q, k, v)


def _gelu_tanh(t):
    return 0.5 * t * (1.0 + jnp.tanh(0.7978845608028654 * (t + 0.044715 * t * t * t)))


def _lru_kernel(*refs, reverse, final, n_chunks):
    if final:
        (xr_ref, xp_ref, xn_ref, hf_ref, yr_ref, shift_ref, cw_ref, cb_ref, wa_ref, wx_ref, ba_ref, bx_ref, lam_ref,
         out_ref, a_s, u_s, h_s, state_s) = refs
    else:
        (xr_ref, xp_ref, xn_ref, shift_ref, cw_ref, cb_ref, wa_ref, wx_ref, ba_ref, bx_ref, lam_ref,
         out_ref, a_s, u_s, h_s, state_s) = refs
    nb, tc, ch = xr_ref.shape
    n_slabs = ch // LANES
    step = pl.program_id(1)
    chunk = (n_chunks - 1 - step) if reverse else step

    @pl.when(step == 0)
    def _():
        state_s[...] = jnp.zeros_like(state_s)

    lam = lam_ref[...]
    neg_c_softplus = -LRU_C * (jnp.maximum(-lam, 0.0) + jnp.log(1.0 + jnp.exp(-jnp.abs(lam))))
    half_log2_decay = (0.5 * LOG2_E) * neg_c_softplus
    pad_rows = jnp.zeros((shift_ref.shape[1] - tc - 2 * HALO, ch), BF16)

    def coeffs(b, carry):
        before = jnp.where(chunk > 0, xp_ref[b], jnp.zeros((HALO, ch), BF16))
        after = jnp.where(chunk < n_chunks - 1, xn_ref[b], jnp.zeros((HALO, ch), BF16))
        stack = jnp.concatenate([before, xr_ref[b], after, pad_rows], axis=0)
        slab_rows = pl.ds(b, tc, stride=nb)
        n_tiles = ch // MXU_DIM
        tile_cols = [slice(kk * MXU_DIM, (kk + 1) * MXU_DIM) for kk in range(n_tiles)]
        xcs = []
        for cols in tile_cols:
            taps = jnp.dot(shift_ref[...], stack[:, cols], preferred_element_type=F32)
            xc = cb_ref[:, cols]
            for j in range(CONV_WIDTH):
                xc = xc + cw_ref[j:j + 1, cols] * taps[j * tc:(j + 1) * tc]
            xcs.append(xc)
        pre = []
        for kk in range(n_tiles):
            xcb = xcs[kk].astype(BF16)
            pre.append((jnp.dot(xcb, wa_ref[kk], preferred_element_type=F32),
                        jnp.dot(xcb, wx_ref[kk], preferred_element_type=F32)))
        for kk, cols in enumerate(tile_cols):
            tanh_r = jnp.tanh(pre[kk][0] + ba_ref[:, cols])
            tanh_i = jnp.tanh(pre[kk][1] + bx_ref[:, cols])
            a = jnp.exp2(tanh_r * half_log2_decay[:, cols] + half_log2_decay[:, cols])
            v = 1.0 - a * a
            root = v * lax.rsqrt(jnp.maximum(v, TINY))
            half_x = 0.5 * xcs[kk]
            u = root * (tanh_i * half_x + half_x)
            for t in range(MXU_DIM // LANES):
                j = kk * (MXU_DIM // LANES) + t
                a_s[j, slab_rows, :] = a[:, t * LANES:(t + 1) * LANES]
                u_s[j, slab_rows, :] = u[:, t * LANES:(t + 1) * LANES]
        return carry

    lax.fori_loop(0, nb, coeffs, 0, unroll=2 if nb % 2 == 0 else 1)

    def scan_step(i, hs):
        t = (tc - 1 - i) if reverse else i
        new = []
        for j in range(n_slabs):
            rows = pl.ds(pl.multiple_of(t * nb, nb), nb)
            hj = a_s[j, rows, :] * hs[j] + u_s[j, rows, :]
            h_s[j, rows, :] = hj
            new.append(hj)
        return tuple(new)

    hs = lax.fori_loop(0, tc, scan_step, tuple(state_s[j] for j in range(n_slabs)), unroll=2)
    for j in range(n_slabs):
        state_s[j] = hs[j]

    def emit(b, carry):
        hb = jnp.concatenate([h_s[j, pl.ds(b, tc, stride=nb), :] for j in range(n_slabs)], axis=1)
        if final:
            hb = (hb + hf_ref[b].astype(F32)) * _gelu_tanh(yr_ref[b].astype(F32))
        out_ref[b] = hb.astype(out_ref.dtype)
        return carry

    lax.fori_loop(0, nb, emit, 0)


def _block_diag_tiles(w):
    per_tile = MXU_DIM // w.shape[1]
    n_tiles = w.shape[0] // per_tile
    w = w.reshape(n_tiles, per_tile, w.shape[1], w.shape[2])
    eye = jnp.eye(per_tile, dtype=w.dtype)
    tiles = jnp.einsum('tpkj,pq->tpkqj', w, eye)
    return tiles.reshape(n_tiles, MXU_DIM, MXU_DIM).astype(BF16)


def _lru_pass(xr, conv_w, conv_b, w_a, b_a, w_x, b_x, lam, *, reverse, h_fwd=None, yr=None):
    bsz, s, ch = xr.shape
    nb = min(LRU_NB, bsz)
    tc = min(LRU_TC, s)
    n_chunks = s // tc
    final = h_fwd is not None
    chunk_of = (lambda c: n_chunks - 1 - c) if reverse else (lambda c: c)
    per_halo = tc // HALO
    last_halo = s // HALO - 1
    main = pl.BlockSpec((nb, tc, ch), lambda g, c: (g, chunk_of(c), 0))
    prev = pl.BlockSpec((nb, HALO, ch), lambda g, c: (g, jnp.maximum(chunk_of(c) * per_halo - 1, 0), 0))
    nxt = pl.BlockSpec((nb, HALO, ch), lambda g, c: (g, jnp.minimum((chunk_of(c) + 1) * per_halo, last_halo), 0))
    const2 = lambda g, c: (0, 0)
    const3 = lambda g, c: (0, 0, 0)
    vec = pl.BlockSpec((1, ch), const2)
    tiles = pl.BlockSpec((ch // MXU_DIM, MXU_DIM, MXU_DIM), const3)
    acts = [xr, xr, xr] + ([h_fwd, yr] if final else [])
    act_specs = [main, prev, nxt] + ([main, main] if final else [])
    slab = pltpu.VMEM((ch // LANES, tc * nb, LANES), F32)
    stack_rows = -(-(tc + 2 * HALO) // MXU_DIM) * MXU_DIM
    tap_row = HALO - CONV_WIDTH // 2 + jnp.arange(CONV_WIDTH)[:, None] + jnp.arange(tc)[None, :]
    shifts = (tap_row.reshape(-1, 1) == jnp.arange(stack_rows)[None, :]).astype(BF16)
    return pl.pallas_call(
        functools.partial(_lru_kernel, reverse=reverse, final=final, n_chunks=n_chunks),
        grid=(bsz // nb, n_chunks),
        in_specs=act_specs + [pl.BlockSpec(shifts.shape, const2), pl.BlockSpec((CONV_WIDTH, ch), const2), vec,
                              tiles, tiles, vec, vec, vec],
        out_specs=main,
        out_shape=jax.ShapeDtypeStruct((bsz, s, ch), BF16),
        scratch_shapes=[slab, slab, slab, pltpu.VMEM((ch // LANES, nb, LANES), F32)],
        compiler_params=_params("arbitrary", "arbitrary"),
        name="lru_bwd" if reverse else "lru_fwd",
    )(*acts, shifts, conv_w.reshape(CONV_WIDTH, ch), conv_b.reshape(1, ch), _block_diag_tiles(0.5 * w_a),
      _block_diag_tiles(0.5 * w_x), (0.5 * b_a).reshape(1, ch), (0.5 * b_x).reshape(1, ch), lam.reshape(1, ch))


def _load_classes(in_ref, row0, n, relayout_s, dilation):
    if dilation == 1:
        return in_ref[0, row0:row0 + n, :].astype(F32)
    n_tiles = GROUP_WIDTH // LANES
    rows = n // dilation
    for r in range(dilation):
        for t in range(n_tiles):
            lo = r * GROUP_WIDTH + t * LANES
            relayout_s[t, pl.ds(r, rows, stride=dilation), :] = (
                in_ref[0, row0 // dilation:row0 // dilation + rows, lo:lo + LANES].astype(F32))
    return jnp.concatenate([relayout_s[t] for t in range(n_tiles)], axis=1)


def _merge_kernel(o0_ref, o1_ref, o2_ref, l0_ref, l1_ref, l2_ref, lru_ref, ga_ref, gl_ref, x_ref, gate_ref,
                  wab_ref, wlb_ref, wout_ref, nw_ref, shift_ref, scale_ref, wr_ref,
                  x1_ref, h2_ref, aff_ref, so1_s, so2_s, sl1_s, sl2_s):
    dil = [dilation for _, dilation in ATTN_GROUPS]
    sub = x_ref.shape[1] // SUB_TILES
    for part in range(SUB_TILES):
        row0 = part * sub
        rows = slice(row0, row0 + sub)
        l0 = _load_classes(l0_ref, row0, sub, None, dil[0])
        l1 = _load_classes(l1_ref, row0, sub, sl1_s.at[part], dil[1])
        l2 = _load_classes(l2_ref, row0, sub, sl2_s.at[part], dil[2])
        o0 = _load_classes(o0_ref, row0, sub, None, dil[0])
        o1 = _load_classes(o1_ref, row0, sub, so1_s.at[part], dil[1])
        o2 = _load_classes(o2_ref, row0, sub, so2_s.at[part], dil[2])
        m = jnp.maximum(jnp.maximum(l0, l1), l2)
        e0, e1, e2 = jnp.exp(l0 - m), jnp.exp(l1 - m), jnp.exp(l2 - m)
        attn = (e0 * o0 + e1 * o1 + e2 * o2) / (e0 + e1 + e2)
        branch_a = jnp.dot(attn.astype(BF16), wab_ref[...], preferred_element_type=F32)
        branch_l = jnp.dot(lru_ref[0, rows, :], wlb_ref[...], preferred_element_type=F32)
        merged = (_sigmoid(ga_ref[0, rows, :].astype(F32)) * branch_a
                  + _sigmoid(gl_ref[0, rows, :].astype(F32)) * branch_l)
        mix = jnp.dot(merged.astype(BF16), wout_ref[...], preferred_element_type=F32)
        x1 = x_ref[0, rows, :] + gate_ref[0] * mix
        x1_ref[0, rows, :] = x1
        ms = jnp.mean(x1 * x1, axis=-1, keepdims=True)
        h2 = x1 * lax.rsqrt(ms + NORM_EPS) * nw_ref[...]
        h2 = h2 * (1.0 + scale_ref[0]) + shift_ref[0]
        h2_ref[0, rows, :] = h2.astype(h2_ref.dtype)
        logits = lax.dot_general(wr_ref[...], h2, (((1,), (1,)), ((), ())), preferred_element_type=F32,
                                 precision=lax.Precision.HIGHEST)
        z = jnp.exp(logits - jnp.max(logits, axis=0, keepdims=True))
        aff_ref[0, :, rows] = z / jnp.sum(z, axis=0, keepdims=True)


def _merge(outs, lses, lru, ga, gl, x, gate1, w_ab, w_lb, w_out, norm_w, shift2, scale2, w_router):
    bsz, s, d = x.shape
    tm = min(ROW_TILE, s)
    n_exp = w_router.shape[1]
    row = lambda b, i: (b, i, 0)
    per_batch = lambda b, i: (b, 0, 0)
    const = lambda b, i: (0, 0)
    grp = [pl.BlockSpec((1, tm // dil, dil * GROUP_WIDTH), row) for _, dil in ATTN_GROUPS]
    wide = pl.BlockSpec((1, tm, d), row)
    mod = pl.BlockSpec((1, 1, d), per_batch)
    full = lambda a: pl.BlockSpec(a.shape, const)
    w_ab, w_lb, w_out = w_ab.astype(BF16), w_lb.astype(BF16), w_out.astype(BF16)
    w_rt = w_router.T
    return pl.pallas_call(
        _merge_kernel,
        grid=(bsz, s // tm),
        in_specs=grp * 2 + [wide] * 4 + [mod, full(w_ab), full(w_lb), full(w_out),
                                          pl.BlockSpec((1, d), const), mod, mod, full(w_rt)],
        out_specs=[wide, wide, pl.BlockSpec((1, n_exp, tm), lambda b, i: (b, 0, i))],
        out_shape=[jax.ShapeDtypeStruct((bsz, s, d), F32), jax.ShapeDtypeStruct((bsz, s, d), BF16),
                   jax.ShapeDtypeStruct((bsz, n_exp, s), F32)],
        scratch_shapes=[pltpu.VMEM((SUB_TILES, GROUP_WIDTH // LANES, tm // SUB_TILES, LANES), F32)] * 4,
        compiler_params=_params("arbitrary", "arbitrary"),
        name="merge",
    )(*outs, *lses, lru, ga, gl, x, gate1.reshape(bsz, 1, d), w_ab, w_lb, w_out, norm_w.reshape(1, d),
      shift2.reshape(bsz, 1, d), scale2.reshape(bsz, 1, d), w_rt)


def _prefix_counts(flags, strict_upper):
    n_exp, s = flags.shape
    n_tiles = s // MXU_DIM
    stacked = jnp.concatenate([flags[:, k * MXU_DIM:(k + 1) * MXU_DIM] for k in range(n_tiles)], axis=0)
    within = jnp.dot(stacked.astype(BF16), strict_upper, preferred_element_type=F32)
    totals = jnp.sum(stacked, axis=1, keepdims=True)
    run = jnp.zeros((n_exp, 1), F32)
    pieces, bases = [], []
    for k in range(n_tiles):
        bases.append(run)
        pieces.append(within[k * n_exp:(k + 1) * n_exp] + run)
        run = run + totals[k * n_exp:(k + 1) * n_exp]
    bases.append(run)
    return jnp.concatenate(pieces, axis=1), bases


def _route_kernel(aff_ref, upper_ref, slot_ref, base_ref, *, capacity):
    aff = aff_ref[0]
    cap = jnp.float32(capacity)

    def refine(i, thr_bits):
        cand = thr_bits | jnp.left_shift(jnp.int32(1), 30 - i)
        cnt = jnp.sum(jnp.where(aff >= pltpu.bitcast(cand, F32), 1.0, 0.0), axis=1, keepdims=True)
        return jnp.where(cnt >= cap, cand, thr_bits)

    thr_bits = lax.fori_loop(0, 31, refine, jnp.zeros((aff.shape[0], 1), jnp.int32))
    thr = pltpu.bitcast(thr_bits, F32)
    above = jnp.where(aff > thr, 1.0, 0.0)
    tied = jnp.where(aff == thr, 1.0, 0.0)
    need = cap - jnp.sum(above, axis=1, keepdims=True)
    upper = upper_ref[...]
    tie_rank, _ = _prefix_counts(tied, upper)
    chosen = above + tied * jnp.where(tie_rank < need, 1.0, 0.0)
    slot, bases = _prefix_counts(chosen, upper)
    slot_ref[0] = jnp.where(chosen > 0.0, slot, -1.0).astype(jnp.int32)
    lane = lax.broadcasted_iota(jnp.int32, (aff.shape[0], LANES), 1)
    table = jnp.zeros((aff.shape[0], LANES), F32)
    for k, bk in enumerate(bases):
        table = jnp.where(lane == k, bk, table)
    base_ref[0] = table.astype(jnp.int32)


def _route(aff, capacity):
    bsz, n_exp, s = aff.shape
    idx = jnp.arange(MXU_DIM)
    upper = (idx[:, None] < idx[None, :]).astype(BF16)
    per_batch = lambda b: (b, 0, 0)
    return pl.pallas_call(
        functools.partial(_route_kernel, capacity=capacity),
        grid=(bsz,),
        in_specs=[pl.BlockSpec((1, n_exp, s), per_batch), pl.BlockSpec((MXU_DIM, MXU_DIM), lambda b: (0, 0))],
        out_specs=[pl.BlockSpec((1, n_exp, s), per_batch), pl.BlockSpec((1, n_exp, LANES), per_batch)],
        out_shape=[jax.ShapeDtypeStruct((bsz, n_exp, s), jnp.int32),
                   jax.ShapeDtypeStruct((bsz, n_exp, LANES), jnp.int32)],
        compiler_params=_params("arbitrary"),
        name="route",
    )(aff, upper)


def _window_plan(base_ref, row, chunk, capacity):
    first = base_ref[row + chunk]
    end = base_ref[row + chunk + 1]
    start = jnp.minimum((first // ROUTE_ALIGN) * ROUTE_ALIGN, capacity - ROUTE_WINDOW)
    n_windows = (end - start + ROUTE_WINDOW - 1) // ROUTE_WINDOW
    return pl.multiple_of(start, ROUTE_ALIGN), n_windows


def _later_window(start, k, capacity):
    lo = start + k * ROUTE_WINDOW
    return lo, pl.multiple_of(jnp.minimum(lo, capacity - ROUTE_WINDOW), ROUTE_ALIGN)


def _dispatch_kernel(base_ref, h_ref, slot_ref, aff_ref, xe_ref, gs_ref, *, n_exp, n_chunks, capacity):
    b, eg, c = pl.program_id(0), pl.program_id(1), pl.program_id(2)
    group = xe_ref.shape[0]
    chunk = h_ref.shape[1]

    @pl.when(c == 0)
    def _():
        xe_ref[...] = jnp.zeros_like(xe_ref)
        gs_ref[...] = jnp.zeros_like(gs_ref)

    h = h_ref[0]
    w_iota = lax.broadcasted_iota(jnp.int32, (ROUTE_WINDOW, chunk), 0)
    plans, hots = [], []
    for e in range(group):
        row = (b * n_exp + eg * group + e) * (n_chunks + 1)
        start, n_windows = _window_plan(base_ref, row, c, capacity)
        plans.append((start, n_windows))
        hots.append((slot_ref[0, e:e + 1, :] - start) == w_iota)
    stack = jnp.concatenate([jnp.where(hot, 1.0, 0.0) for hot in hots], axis=0).astype(BF16)
    rows = jnp.dot(stack, h, preferred_element_type=F32)
    most_windows = plans[0][1]
    for e in range(group):
        start, n_windows = plans[e]
        most_windows = jnp.maximum(most_windows, n_windows)
        win = pl.ds(start, ROUTE_WINDOW)
        xe_ref[e, 0, win, :] += rows[e * ROUTE_WINDOW:(e + 1) * ROUTE_WINDOW].astype(xe_ref.dtype)
        gs_ref[e, 0, win, :] += jnp.sum(jnp.where(hots[e], aff_ref[0, e:e + 1, :], 0.0), axis=1, keepdims=True)

    @pl.when(most_windows > 1)
    def _():
        for e in range(group):
            start, n_windows = plans[e]

            def more(k, carry, e=e, start=start):
                lo, st = _later_window(start, k, capacity)
                slots = slot_ref[0, e:e + 1, :]
                hot = jnp.logical_and(slots - st == w_iota, slots >= lo)
                extra = jnp.dot(jnp.where(hot, 1.0, 0.0).astype(BF16), h, preferred_element_type=F32)
                xe_ref[e, 0, pl.ds(st, ROUTE_WINDOW), :] += extra.astype(xe_ref.dtype)
                gs_ref[e, 0, pl.ds(st, ROUTE_WINDOW), :] += jnp.sum(
                    jnp.where(hot, aff_ref[0, e:e + 1, :], 0.0), axis=1, keepdims=True)
                return carry

            lax.fori_loop(1, n_windows, more, 0)


def _dispatch(h2, slot, aff, base_flat, capacity):
    bsz, s, d = h2.shape
    n_exp = slot.shape[1]
    chunk = min(ROUTE_CHUNK, s)
    group = min(EXPERT_GROUP, n_exp)
    n_chunks = s // chunk
    return pl.pallas_call(
        functools.partial(_dispatch_kernel, n_exp=n_exp, n_chunks=n_chunks, capacity=capacity),
        grid_spec=pltpu.PrefetchScalarGridSpec(
            num_scalar_prefetch=1,
            grid=(bsz, n_exp // group, n_chunks),
            in_specs=[pl.BlockSpec((1, chunk, d), lambda b, g, c, base: (b, c, 0)),
                      pl.BlockSpec((1, group, chunk), lambda b, g, c, base: (b, g, c)),
                      pl.BlockSpec((1, group, chunk), lambda b, g, c, base: (b, g, c))],
            out_specs=[pl.BlockSpec((group, 1, capacity, d), lambda b, g, c, base: (g, b, 0, 0)),
                       pl.BlockSpec((group, 1, capacity, 1), lambda b, g, c, base: (g, b, 0, 0))]),
        out_shape=[jax.ShapeDtypeStruct((n_exp, bsz, capacity, d), BF16),
                   jax.ShapeDtypeStruct((n_exp, bsz, capacity, 1), F32)],
        compiler_params=_params("arbitrary", "arbitrary", "arbitrary"),
        name="dispatch",
    )(base_flat, h2, slot, aff)


def _expert_kernel(x_ref, g_ref, wg_ref, wu_ref, wd_ref, y_ref):
    x = x_ref[0, 0]
    gate = jnp.dot(x, wg_ref[0], preferred_element_type=F32)
    up = jnp.dot(x, wu_ref[0], preferred_element_type=F32)
    he = (gate * _sigmoid(gate) * up).astype(BF16)
    y_ref[0, 0] = (jnp.dot(he, wd_ref[0], preferred_element_type=F32) * g_ref[0, 0]).astype(y_ref.dtype)


def _experts(xe, gates, w_gate, w_up, w_down):
    n_exp, bsz, cap, d = xe.shape
    ff = w_gate.shape[2]
    row = lambda e, i: (e, i, 0, 0)
    per_expert = lambda e, i: (e, 0, 0)
    return pl.pallas_call(
        _expert_kernel,
        grid=(n_exp, bsz),
        in_specs=[pl.BlockSpec((1, 1, cap, d), row), pl.BlockSpec((1, 1, cap, 1), row),
                  pl.BlockSpec((1, d, ff), per_expert), pl.BlockSpec((1, d, ff), per_expert),
                  pl.BlockSpec((1, ff, d), per_expert)],
        out_specs=pl.BlockSpec((1, 1, cap, d), row),
        out_shape=jax.ShapeDtypeStruct(xe.shape, BF16),
        compiler_params=_params("arbitrary", "arbitrary"),
        name="experts",
    )(xe, gates, w_gate.astype(BF16), w_up.astype(BF16), w_down.astype(BF16))


def _combine_kernel(base_ref, y_ref, slot_ref, x1_ref, gate_ref, out_ref, acc_s, *, n_chunks, capacity):
    b, c = pl.program_id(0), pl.program_id(1)
    n_exp = y_ref.shape[0]
    chunk = x1_ref.shape[1]
    w_iota = lax.broadcasted_iota(jnp.int32, (ROUTE_WINDOW, chunk), 0)
    plans, hots, wins = [], [], []
    for e in range(n_exp):
        start, n_windows = _window_plan(base_ref, (b * n_exp + e) * (n_chunks + 1), c, capacity)
        plans.append((start, n_windows))
        hots.append(jnp.where((slot_ref[0, e:e + 1, :] - start) == w_iota, 1.0, 0.0))
        wins.append(y_ref[e, 0, pl.ds(start, ROUTE_WINDOW), :])
    hot = jnp.concatenate(hots, axis=0).astype(BF16)
    ywin = jnp.concatenate(wins, axis=0)
    acc = lax.dot_general(hot, ywin, (((0,), (0,)), ((), ())), preferred_element_type=F32)
    most_windows = plans[0][1]
    for _, n_windows in plans[1:]:
        most_windows = jnp.maximum(most_windows, n_windows)

    @pl.when(most_windows <= 1)
    def _():
        out_ref[0] = x1_ref[0] + gate_ref[0] * acc

    @pl.when(most_windows > 1)
    def _():
        acc_s[...] = acc
        for e in range(n_exp):
            start, n_windows = plans[e]

            def more(k, carry, e=e, start=start):
                lo, st = _later_window(start, k, capacity)
                slots = slot_ref[0, e:e + 1, :]
                sel = jnp.logical_and(slots - st == w_iota, slots >= lo)
                acc_s[...] += lax.dot_general(jnp.where(sel, 1.0, 0.0).astype(BF16),
                                              y_ref[e, 0, pl.ds(st, ROUTE_WINDOW), :],
                                              (((0,), (0,)), ((), ())), preferred_element_type=F32)
                return carry

            lax.fori_loop(1, n_windows, more, 0)
        out_ref[0] = x1_ref[0] + gate_ref[0] * acc_s[...]


def _combine(ys, slot, base_flat, x1, gate2, capacity):
    bsz, s, d = x1.shape
    n_exp = ys.shape[0]
    chunk = min(ROUTE_CHUNK, s)
    n_chunks = s // chunk
    return pl.pallas_call(
        functools.partial(_combine_kernel, n_chunks=n_chunks, capacity=capacity),
        grid_spec=pltpu.PrefetchScalarGridSpec(
            num_scalar_prefetch=1,
            grid=(bsz, n_chunks),
            in_specs=[pl.BlockSpec((n_exp, 1, capacity, d), lambda b, c, base: (0, b, 0, 0),
                                   pipeline_mode=pl.Buffered(1)),
                      pl.BlockSpec((1, n_exp, chunk), lambda b, c, base: (b, 0, c)),
                      pl.BlockSpec((1, chunk, d), lambda b, c, base: (b, c, 0)),
                      pl.BlockSpec((1, 1, d), lambda b, c, base: (b, 0, 0))],
            out_specs=pl.BlockSpec((1, chunk, d), lambda b, c, base: (b, c, 0)),
            scratch_shapes=[pltpu.VMEM((chunk, d), F32)]),
        out_shape=jax.ShapeDtypeStruct(x1.shape, F32),
        compiler_params=_params("arbitrary", "arbitrary"),
        name="combine",
    )(base_flat, ys, slot, x1, gate2.reshape(bsz, 1, d))


def _layer(x, c, positions, w_ada, b_ada, norm1_w, w_in, q_norm_w, k_norm_w, conv_w, conv_b, lru_w_a, lru_b_a,
           lru_w_x, lru_b_x, lru_lambda, w_attn_branch, w_lru_branch, w_out, norm2_w, w_router, w_gate, w_up,
           w_down):
    bsz, s, d = x.shape
    mod = _ada(c, w_ada, b_ada)
    shift1, scale1, gate1, shift2, scale2, gate2 = jnp.split(mod, 6, axis=-1)

    (q0, q1, q2, k0, k1, k2, v0, v1, v2, xr, yr, ga, gl) = _inproj(x, positions, shift1, scale1, norm1_w, w_in,
                                                                  q_norm_w, k_norm_w)

    outs, lses = [], []
    for (window, dilation), q, k, v in zip(ATTN_GROUPS, (q0, q1, q2), (k0, k1, k2), (v0, v1, v2)):
        o, lse = _attention_group(q, k, v, window, dilation)
        outs.append(o)
        lses.append(lse)

    h_fwd = _lru_pass(xr, conv_w, conv_b, lru_w_a[0], lru_b_a[0], lru_w_x[0], lru_b_x[0], lru_lambda[0],
                      reverse=False)
    lru = _lru_pass(xr, conv_w, conv_b, lru_w_a[1], lru_b_a[1], lru_w_x[1], lru_b_x[1], lru_lambda[1],
                    reverse=True, h_fwd=h_fwd, yr=yr)

    x1, h2, aff = _merge(outs, lses, lru, ga, gl, x, gate1, w_attn_branch, w_lru_branch, w_out, norm2_w,
                         shift2, scale2, w_router)

    capacity = max(1, CAPACITY_FACTOR * s // N_EXPERTS)
    n_chunks = s // min(ROUTE_CHUNK, s)
    slot, base_table = _route(aff, capacity)
    base_flat = base_table[:, :, :n_chunks + 1].reshape(-1)
    xe, gates = _dispatch(h2, slot, aff, base_flat, capacity)
    ys = _experts(xe, gates, w_gate, w_up, w_down)
    return _combine(ys, slot, base_flat, x1, gate2, capacity)


def kernel(x, c, positions, w_ada, b_ada, norm1_w, w_in, q_norm_w, k_norm_w, conv_w, conv_b, lru_w_a, lru_b_a,
           lru_w_x, lru_b_x, lru_lambda, w_attn_branch, w_lru_branch, w_out, norm2_w, w_router, w_gate, w_up,
           w_down):
    for l in range(w_ada.shape[0]):
        x = _layer(x, c, positions, w_ada[l], b_ada[l], norm1_w[l], w_in[l], q_norm_w[l], k_norm_w[l],
                   conv_w[l, :, 0, :], conv_b[l], lru_w_a[l], lru_b_a[l], lru_w_x[l], lru_b_x[l], lru_lambda[l],
                   w_attn_branch[l], w_lru_branch[l], w_out[l], norm2_w[l], w_router[l], w_gate[l], w_up[l],
                   w_down[l])
    return x
```

```python
import functools

import jax
import jax.numpy as jnp
from jax import lax
from jax.experimental import pallas as pl
from jax.experimental.pallas import tpu as pltpu

F32 = jnp.float32
BF16 = jnp.bfloat16

HEAD_DIM = 64
HEADS_PER_GROUP = 4
GROUP_WIDTH = HEADS_PER_GROUP * HEAD_DIM
ATTN_GROUPS = ((128, 1), (512, 4), (2048, 16))
ATTN_WIDTH = GROUP_WIDTH * len(ATTN_GROUPS)
LRU_BLOCKS = 16
LRU_C = 8.0
CONV_WIDTH = 4
N_EXPERTS = 16
CAPACITY_FACTOR = 2
ROPE_THETA = 10000.0
NORM_EPS = 1e-6
NEG_INF = -1e30
LOG2_E = 1.4426950408889634
TINY = 1e-30

LANES = 128
SUBLANES = 8
MXU_DIM = 256
VMEM_LIMIT = 56 * 1024 * 1024

ROW_TILE = 512
SUB_TILES = 1
ATTN_TQ = 128
ATTN_CLASSES_PER_STEP = 4
LRU_TC = 128
LRU_NB = 8
HALO = 16
ROUTE_CHUNK = 256
ROUTE_WINDOW = 64
ROUTE_ALIGN = 16
EXPERT_GROUP = 16
ROPE_PACK = LANES // (HEAD_DIM // 2)


def _sigmoid(t):
    return 0.5 * jnp.tanh(0.5 * t) + 0.5


def _params(*sem):
    return pltpu.CompilerParams(dimension_semantics=sem, vmem_limit_bytes=VMEM_LIMIT)


def _ada_kernel(c_ref, w_ref, b_ref, o_ref):
    c = c_ref[...]
    o_ref[...] = jnp.dot(c * _sigmoid(c), w_ref[...], preferred_element_type=F32,
                         precision=lax.Precision.HIGHEST) + b_ref[...]


def _ada(c, w, b):
    bsz, d = c.shape
    n = w.shape[1]
    tn = n // 4
    return pl.pallas_call(
        _ada_kernel,
        grid=(n // tn,),
        in_specs=[pl.BlockSpec((bsz, d), lambda j: (0, 0)),
                  pl.BlockSpec((d, tn), lambda j: (0, j)),
                  pl.BlockSpec((1, tn), lambda j: (0, j))],
        out_specs=pl.BlockSpec((bsz, tn), lambda j: (0, j)),
        out_shape=jax.ShapeDtypeStruct((bsz, n), F32),
        compiler_params=_params("arbitrary"),
        name="ada",
    )(c, w, b.reshape(1, n))


def _store_classes(val, out_ref, row0, relayout_s, dilation):
    n = val.shape[0]
    if dilation == 1:
        out_ref[0, row0:row0 + n, :] = val.astype(out_ref.dtype)
        return
    n_tiles = GROUP_WIDTH // LANES
    rows = n // dilation
    for t in range(n_tiles):
        relayout_s[t] = val[:, t * LANES:(t + 1) * LANES]
    for r in range(dilation):
        for t in range(n_tiles):
            lo = r * GROUP_WIDTH + t * LANES
            out_ref[0, row0 // dilation:row0 // dilation + rows, lo:lo + LANES] = (
                relayout_s[t, pl.ds(r, rows, stride=dilation), :].astype(out_ref.dtype))


def _inproj_kernel(x_ref, pos_ref, shift_ref, scale_ref, nw_ref, w_ref, qnw_ref, knw_ref, invf_ref, seg_ref,
                   spread_ref, q0_ref, q1_ref, q2_ref, k0_ref, k1_ref, k2_ref, v0_ref, v1_ref, v2_ref,
                   xr_ref, yr_ref, ga_ref, gl_ref, relayout_s, cos_s, sin_s):
    tm, d = x_ref.shape[1:]
    sub = tm // SUB_TILES
    lane = lax.broadcasted_iota(jnp.int32, (sub, GROUP_WIDTH), 1)
    first_half = (lane & (HEAD_DIM // 2)) == 0
    seg = seg_ref[...]

    for part in range(SUB_TILES):
        row0 = part * sub
        x = x_ref[0, row0:row0 + sub, :]
        ms = jnp.mean(x * x, axis=-1, keepdims=True)
        h = x * lax.rsqrt(ms + NORM_EPS) * nw_ref[...]
        hb = (h * (1.0 + scale_ref[0]) + shift_ref[0]).astype(BF16)

        packed_rows = sub // ROPE_PACK
        ang = pos_ref[0, row0 // ROPE_PACK:row0 // ROPE_PACK + packed_rows, :].astype(F32) * invf_ref[...]
        cos_p = jnp.cos(ang)
        sin_p = jnp.sin(ang)
        for p in range(ROPE_PACK):
            rows_p = pl.ds(p, packed_rows, stride=ROPE_PACK)
            cos_s[rows_p, :] = jnp.dot(cos_p, spread_ref[p], preferred_element_type=F32,
                                       precision=lax.Precision.HIGHEST)
            sin_s[rows_p, :] = jnp.dot(sin_p, spread_ref[p], preferred_element_type=F32,
                                       precision=lax.Precision.HIGHEST)
        cos = cos_s[...]
        sin = sin_s[...]
        cos2 = jnp.concatenate([cos, cos], axis=1)
        sin2 = jnp.concatenate([sin, sin], axis=1)
        sin2 = jnp.where(first_half, -sin2, sin2)
        scratch = relayout_s.at[part]

        def norm_rope(off, w_norm_ref, out_ref, j):
            lo_c = j * GROUP_WIDTH
            t = jnp.dot(hb, w_ref[:, off + lo_c:off + lo_c + GROUP_WIDTH], preferred_element_type=F32)
            msq = jnp.dot((t * t).astype(BF16), seg, preferred_element_type=F32) * (1.0 / HEAD_DIM)
            y = t * lax.rsqrt(msq + NORM_EPS) * w_norm_ref[:, lo_c:lo_c + GROUP_WIDTH]
            rot = jnp.where(first_half, pltpu.roll(y, GROUP_WIDTH - HEAD_DIM // 2, 1),
                            pltpu.roll(y, HEAD_DIM // 2, 1))
            _store_classes(y * cos2 + rot * sin2, out_ref, row0, scratch, ATTN_GROUPS[j][1])

        def value(out_ref, j):
            off = 2 * ATTN_WIDTH + j * GROUP_WIDTH
            v = jnp.dot(hb, w_ref[:, off:off + GROUP_WIDTH], preferred_element_type=F32)
            _store_classes(v, out_ref, row0, scratch, ATTN_GROUPS[j][1])

        def wide(out_a_ref, out_b_ref, i):
            off = 3 * ATTN_WIDTH + i * d
            t = jnp.dot(hb, w_ref[:, off:off + 2 * d], preferred_element_type=F32)
            out_a_ref[0, row0:row0 + sub, :] = t[:, :d].astype(out_a_ref.dtype)
            out_b_ref[0, row0:row0 + sub, :] = t[:, d:].astype(out_b_ref.dtype)

        wide(xr_ref, yr_ref, 0)
        norm_rope(0, qnw_ref, q0_ref, 0)
        norm_rope(ATTN_WIDTH, knw_ref, k0_ref, 0)
        wide(ga_ref, gl_ref, 2)
        norm_rope(0, qnw_ref, q1_ref, 1)
        value(v0_ref, 0)
        norm_rope(ATTN_WIDTH, knw_ref, k1_ref, 1)
        value(v1_ref, 1)
        norm_rope(0, qnw_ref, q2_ref, 2)
        value(v2_ref, 2)
        norm_rope(ATTN_WIDTH, knw_ref, k2_ref, 2)


def _inproj(x, positions, shift, scale, norm_w, w_in, q_norm_w, k_norm_w):
    bsz, s, d = x.shape
    tm = min(ROW_TILE, s)
    half = HEAD_DIM // 2
    inv_freq = ROPE_THETA ** (-jnp.arange(half, dtype=F32) / half)
    inv_freq = jnp.tile(inv_freq, ROPE_PACK).reshape(1, LANES)
    pos_packed = jnp.repeat(positions.reshape(bsz, s // ROPE_PACK, ROPE_PACK), half, axis=-1)
    lane = jnp.arange(LANES)
    spread = (lane[None, :, None] == (jnp.arange(ROPE_PACK)[:, None, None] * half + lane[None, None, :] % half))
    spread = spread.astype(F32)
    head_of = jnp.arange(GROUP_WIDTH) // HEAD_DIM
    seg = (head_of[:, None] == head_of[None, :]).astype(BF16)
    col_scale = jnp.where(jnp.arange(w_in.shape[1]) >= w_in.shape[1] - 2 * d, 0.5, 1.0).astype(F32)
    row = lambda b, i: (b, i, 0)
    per_batch = lambda b, i: (b, 0, 0)
    const = lambda b, i: (0, 0)
    wide =[jax.ShapeDtypeStruct((bsz, s, d), BF16)] * 4
    narrow = [jax.ShapeDtypeStruct((bsz, s // dil, dil * GROUP_WIDTH), BF16) for _, dil in ATTN_GROUPS] * 3
    narrow_specs = [pl.BlockSpec((1, tm // dil, dil * GROUP_WIDTH), row) for _, dil in ATTN_GROUPS] * 3
    return pl.pallas_call(
        _inproj_kernel,
        grid=(bsz, s // tm),
        in_specs=[pl.BlockSpec((1, tm, d), row),
                  pl.BlockSpec((1, tm // ROPE_PACK, LANES), row),
                  pl.BlockSpec((1, 1, d), per_batch),
                  pl.BlockSpec((1, 1, d), per_batch),
                  pl.BlockSpec((1, d), const),
                  pl.BlockSpec(w_in.shape, const),
                  pl.BlockSpec((1, ATTN_WIDTH), const),
                  pl.BlockSpec((1, ATTN_WIDTH), const),
                  pl.BlockSpec((1, LANES), const),
                  pl.BlockSpec((GROUP_WIDTH, GROUP_WIDTH), const),
                  pl.BlockSpec(spread.shape, lambda b, i: (0, 0, 0))],
        out_specs=narrow_specs + [pl.BlockSpec((1, tm, d), row)] * 4,
        out_shape=narrow + wide,
        scratch_shapes=[pltpu.VMEM((SUB_TILES, GROUP_WIDTH // LANES, tm // SUB_TILES, LANES), F32),
                        pltpu.VMEM((tm // SUB_TILES, LANES), F32), pltpu.VMEM((tm // SUB_TILES, LANES), F32)],
        compiler_params=_params("arbitrary", "arbitrary"),
        name="inproj",
    )(x, pos_packed, shift.reshape(bsz, 1, d), scale.reshape(bsz, 1, d),
      norm_w.reshape(1, d), (w_in * col_scale).astype(BF16), (q_norm_w * HEAD_DIM ** -0.5).reshape(1, ATTN_WIDTH),
      k_norm_w.reshape(1, ATTN_WIDTH), inv_freq, seg, spread)


def _attn_kernel(q_ref, k_ref, v_ref, o_ref, lse_ref, *, length, tq, tk, half):
    rel0 = (lax.broadcasted_iota(jnp.int32, (2 * tq, tk), 1)
            - (lax.broadcasted_iota(jnp.int32, (2 * tq, tk), 0) & (tq - 1)))
    lane = lax.broadcasted_iota(jnp.int32, (tq, LANES), 1)
    head_a = lane < HEAD_DIM
    n_tiles = length // tq

    def tile(i, carry, col0):
        t0 = pl.multiple_of(i * tq, tq)
        ws = pl.multiple_of(jnp.clip(t0 - half, 0, length - tk), half)
        valid = jnp.abs(rel0 + (ws - t0)) <= half
        for p in range(GROUP_WIDTH // LANES):
            cols = slice(col0 + p * LANES, col0 + (p + 1) * LANES)
            qp = q_ref[0, pl.ds(t0, tq), cols]
            kp = k_ref[0, pl.ds(ws, tk), cols]
            vp = v_ref[0, pl.ds(ws, tk), cols]
            zero = jnp.zeros_like(qp)
            q2 = jnp.concatenate([jnp.where(head_a, qp, zero), jnp.where(head_a, zero, qp)], axis=0)
            s = lax.dot_general(q2, kp, (((1,), (1,)), ((), ())), preferred_element_type=F32)
            s = jnp.where(valid, s, NEG_INF)
            m = jnp.max(s, axis=1, keepdims=True)
            e = jnp.exp(s - m)
            den = jnp.sum(e, axis=1, keepdims=True)
            o = jnp.dot(e.astype(BF16), vp, preferred_element_type=F32) * (1.0 / den)
            lse = jnp.broadcast_to(m + jnp.log(den), (2 * tq, LANES))
            o_ref[0, pl.ds(t0, tq), cols] = jnp.where(head_a, o[:tq], o[tq:]).astype(o_ref.dtype)
            lse_ref[0, pl.ds(t0, tq), cols] = jnp.where(head_a, lse[:tq], lse[tq:])
        return carry

    unroll = 4 if n_tiles % 4 == 0 else (2 if n_tiles % 2 == 0 else 1)
    for cls in range(q_ref.shape[2] // GROUP_WIDTH):
        lax.fori_loop(0, n_tiles, functools.partial(tile, col0=cls * GROUP_WIDTH), 0, unroll=unroll)


def _attention_group(q, k, v, window, dilation):
    bsz, length, _ = q.shape
    half = window // (2 * dilation)
    tq = min(ATTN_TQ, length)
    assert tq & (tq - 1) == 0, "query tile must be a power of two"
    tk = min(tq + 2 * half, length)
    classes = min(dilation, ATTN_CLASSES_PER_STEP)
    spec = pl.BlockSpec((1, length, classes * GROUP_WIDTH), lambda b, r: (b, 0, r))
    return pl.pallas_call(
        functools.partial(_attn_kernel, length=length, tq=tq, tk=tk, half=half),
        grid=(bsz, dilation // classes),
        in_specs=[spec] * 3,
        out_specs=[spec] * 2,
        out_shape=[jax.ShapeDtypeStruct(q.shape, BF16), jax.ShapeDtypeStruct(q.shape, F32)],
        compiler_params=_params("arbitrary", "arbitrary"),
        name=f"attn_d{dilation}",
    )(q, k, v)


def _gelu_tanh(t):
    return 0.5 * t * (1.0 + jnp.tanh(0.7978845608028654 * (t + 0.044715 * t * t * t)))


def _lru_kernel(*refs, reverse, final, n_chunks):
    if final:
        (xr_ref, xp_ref, xn_ref, hf_ref, yr_ref, shift_ref, cw_ref, cb_ref, wa_ref, wx_ref, ba_ref, bx_ref, lam_ref,
         out_ref, a_s, u_s, h_s, state_s) = refs
    else:
        (xr_ref, xp_ref, xn_ref, shift_ref, cw_ref, cb_ref, wa_ref, wx_ref, ba_ref, bx_ref, lam_ref,
         out_ref, a_s, u_s, h_s, state_s) = refs
    nb, tc, ch = xr_ref.shape
    n_slabs = ch // LANES
    step = pl.program_id(1)
    chunk = (n_chunks - 1 - step) if reverse else step

    @pl.when(step == 0)
    def _():
        state_s[...] = jnp.zeros_like(state_s)

    lam = lam_ref[...]
    neg_c_softplus = -LRU_C * (jnp.maximum(-lam, 0.0) + jnp.log(1.0 + jnp.exp(-jnp.abs(lam))))
    half_log2_decay = (0.5 * LOG2_E) * neg_c_softplus
    pad_rows = jnp.zeros((shift_ref.shape[1] - tc - 2 * HALO, ch), BF16)

    def coeffs(b, carry):
        before = jnp.where(chunk > 0, xp_ref[b], jnp.zeros((HALO, ch), BF16))
        after = jnp.where(chunk < n_chunks - 1, xn_ref[b], jnp.zeros((HALO, ch), BF16))
        stack = jnp.concatenate([before, xr_ref[b], after, pad_rows], axis=0)
        slab_rows = pl.ds(b, tc, stride=nb)
        n_tiles = ch // MXU_DIM
        tile_cols = [slice(kk * MXU_DIM, (kk + 1) * MXU_DIM) for kk in range(n_tiles)]
        xcs = []
        for cols in tile_cols:
            taps = jnp.dot(shift_ref[...], stack[:, cols], preferred_element_type=F32)
            xc = cb_ref[:, cols]
            for j in range(CONV_WIDTH):
                xc = xc + cw_ref[j:j + 1, cols] * taps[j * tc:(j + 1) * tc]
            xcs.append(xc)
        pre = []
        for kk in range(n_tiles):
            xcb = xcs[kk].astype(BF16)
            pre.append((jnp.dot(xcb, wa_ref[kk], preferred_element_type=F32),
                        jnp.dot(xcb, wx_ref[kk], preferred_element_type=F32)))
        for kk, cols in enumerate(tile_cols):
            tanh_r = jnp.tanh(pre[kk][0] + ba_ref[:, cols])
            tanh_i = jnp.tanh(pre[kk][1] + bx_ref[:, cols])
            a = jnp.exp2(tanh_r * half_log2_decay[:, cols] + half_log2_decay[:, cols])
            v = 1.0 - a * a
            root = v * lax.rsqrt(jnp.maximum(v, TINY))
            half_x = 0.5 * xcs[kk]
            u = root * (tanh_i * half_x + half_x)
            for t in range(MXU_DIM // LANES):
                j = kk * (MXU_DIM // LANES) + t
                a_s[j, slab_rows, :] = a[:, t * LANES:(t + 1) * LANES]
                u_s[j, slab_rows, :] = u[:, t * LANES:(t + 1) * LANES]
        return carry

    lax.fori_loop(0, nb, coeffs, 0, unroll=2 if nb % 2 == 0 else 1)

    def scan_step(i, hs):
        t = (tc - 1 - i) if reverse else i
        new = []
        for j in range(n_slabs):
            rows = pl.ds(pl.multiple_of(t * nb, nb), nb)
            hj = a_s[j, rows, :] * hs[j] + u_s[j, rows, :]
            h_s[j, rows, :] = hj
            new.append(hj)
        return tuple(new)

    hs = lax.fori_loop(0, tc, scan_step, tuple(state_s[j] for j in range(n_slabs)), unroll=2)
    for j in range(n_slabs):
        state_s[j] = hs[j]

    def emit(b, carry):
        hb = jnp.concatenate([h_s[j, pl.ds(b, tc, stride=nb), :] for j in range(n_slabs)], axis=1)
        if final:
            hb = (hb + hf_ref[b].astype(F32)) * _gelu_tanh(yr_ref[b].astype(F32))
        out_ref[b] = hb.astype(out_ref.dtype)
        return carry

    lax.fori_loop(0, nb, emit, 0, unroll=2 if nb % 2 == 0 else 1)


def _block_diag_tiles(w):
    per_tile = MXU_DIM // w.shape[1]
    n_tiles = w.shape[0] // per_tile
    w = w.reshape(n_tiles, per_tile, w.shape[1], w.shape[2])
    eye = jnp.eye(per_tile, dtype=w.dtype)
    tiles = jnp.einsum('tpkj,pq->tpkqj', w, eye)
    return tiles.reshape(n_tiles, MXU_DIM, MXU_DIM).astype(BF16)


def _lru_pass(xr, conv_w, conv_b, w_a, b_a, w_x, b_x, lam, *, reverse, h_fwd=None, yr=None):
    bsz, s, ch = xr.shape
    nb = min(LRU_NB, bsz)
    tc = min(LRU_TC, s)
    n_chunks = s // tc
    final = h_fwd is not None
    chunk_of = (lambda c: n_chunks - 1 - c) if reverse else (lambda c: c)
    per_halo = tc // HALO
    last_halo = s // HALO - 1
    main = pl.BlockSpec((nb, tc, ch), lambda g, c: (g, chunk_of(c), 0))
    prev = pl.BlockSpec((nb, HALO, ch), lambda g, c: (g, jnp.maximum(chunk_of(c) * per_halo - 1, 0), 0))
    nxt = pl.BlockSpec((nb, HALO, ch), lambda g, c: (g, jnp.minimum((chunk_of(c) + 1) * per_halo, last_halo), 0))
    const2 = lambda g, c: (0, 0)
    const3 = lambda g, c: (0, 0, 0)
    vec = pl.BlockSpec((1, ch), const2)
    tiles = pl.BlockSpec((ch // MXU_DIM, MXU_DIM, MXU_DIM), const3)
    acts = [xr, xr, xr] + ([h_fwd, yr] if final else [])
    act_specs = [main, prev, nxt] + ([main, main] if final else [])
    slab = pltpu.VMEM((ch // LANES, tc * nb, LANES), F32)
    stack_rows = -(-(tc + 2 * HALO) // MXU_DIM) * MXU_DIM
    tap_row = HALO - CONV_WIDTH // 2 + jnp.arange(CONV_WIDTH)[:, None] + jnp.arange(tc)[None, :]
    shifts = (tap_row.reshape(-1, 1) == jnp.arange(stack_rows)[None, :]).astype(BF16)
    return pl.pallas_call(
        functools.partial(_lru_kernel, reverse=reverse, final=final, n_chunks=n_chunks),
        grid=(bsz // nb, n_chunks),
        in_specs=act_specs + [pl.BlockSpec(shifts.shape, const2), pl.BlockSpec((CONV_WIDTH, ch), const2), vec,
                              tiles, tiles, vec, vec, vec],
        out_specs=main,
        out_shape=jax.ShapeDtypeStruct((bsz, s, ch), BF16),
        scratch_shapes=[slab, slab, slab, pltpu.VMEM((ch // LANES, nb, LANES), F32)],
        compiler_params=_params("arbitrary", "arbitrary"),
        name="lru_bwd" if reverse else "lru_fwd",
    )(*acts, shifts, conv_w.reshape(CONV_WIDTH, ch), conv_b.reshape(1, ch), _block_diag_tiles(0.5 * w_a),
      _block_diag_tiles(0.5 * w_x), (0.5 * b_a).reshape(1, ch), (0.5 * b_x).reshape(1, ch), lam.reshape(1, ch))


def _load_classes(in_ref, row0, n, relayout_s, dilation):
    if dilation == 1:
        return in_ref[0, row0:row0 + n, :].astype(F32)
    n_tiles = GROUP_WIDTH // LANES
    rows = n // dilation
    for r in range(dilation):
        for t in range(n_tiles):
            lo = r * GROUP_WIDTH + t * LANES
            relayout_s[t, pl.ds(r, rows, stride=dilation), :] = (
                in_ref[0, row0 // dilation:row0 // dilation + rows, lo:lo + LANES].astype(F32))
    return jnp.concatenate([relayout_s[t] for t in range(n_tiles)], axis=1)


def _merge_kernel(o0_ref, o1_ref, o2_ref, l0_ref, l1_ref, l2_ref, lru_ref, ga_ref, gl_ref, x_ref, gate_ref,
                  wab_ref, wlb_ref, wout_ref, nw_ref, shift_ref, scale_ref, wr_ref,
                  x1_ref, h2_ref, aff_ref, so1_s, so2_s, sl1_s, sl2_s):
    dil = [dilation for _, dilation in ATTN_GROUPS]
    sub = x_ref.shape[1] // SUB_TILES
    for part in range(SUB_TILES):
        row0 = part * sub
        rows = slice(row0, row0 + sub)
        l0 = _load_classes(l0_ref, row0, sub, None, dil[0])
        l1 = _load_classes(l1_ref, row0, sub, sl1_s.at[part], dil[1])
        l2 = _load_classes(l2_ref, row0, sub, sl2_s.at[part], dil[2])
        o0 = _load_classes(o0_ref, row0, sub, None, dil[0])
        o1 = _load_classes(o1_ref, row0, sub, so1_s.at[part], dil[1])
        o2 = _load_classes(o2_ref, row0, sub, so2_s.at[part], dil[2])
        m = jnp.maximum(jnp.maximum(l0, l1), l2)
        e0, e1, e2 = jnp.exp(l0 - m), jnp.exp(l1 - m), jnp.exp(l2 - m)
        attn = (e0 * o0 + e1 * o1 + e2 * o2) / (e0 + e1 + e2)
        branch_a = jnp.dot(attn.astype(BF16), wab_ref[...], preferred_element_type=F32)
        branch_l = jnp.dot(lru_ref[0, rows, :], wlb_ref[...], preferred_element_type=F32)
        merged = ((jnp.tanh(ga_ref[0, rows, :].astype(F32)) + 1.0) * branch_a
                  + (jnp.tanh(gl_ref[0, rows, :].astype(F32)) + 1.0) * branch_l)
        mix = jnp.dot(merged.astype(BF16), wout_ref[...], preferred_element_type=F32)
        x1 = x_ref[0, rows, :] + gate_ref[0] * mix
        x1_ref[0, rows, :] = x1
        ms = jnp.mean(x1 * x1, axis=-1, keepdims=True)
        h2 = x1 * lax.rsqrt(ms + NORM_EPS) * nw_ref[...]
        h2 = h2 * (1.0 + scale_ref[0]) + shift_ref[0]
        h2_ref[0, rows, :] = h2.astype(h2_ref.dtype)
        logits = lax.dot_general(wr_ref[...], h2, (((1,), (1,)), ((), ())), preferred_element_type=F32,
                                 precision=lax.Precision.HIGHEST)
        z = jnp.exp(logits - jnp.max(logits, axis=0, keepdims=True))
        aff_ref[0, :, rows] = z / jnp.sum(z, axis=0, keepdims=True)


def _merge(outs, lses, lru, ga, gl, x, gate1, w_ab, w_lb, w_out, norm_w, shift2, scale2, w_router):
    bsz, s, d = x.shape
    tm = min(ROW_TILE, s)
    n_exp = w_router.shape[1]
    row = lambda b, i: (b, i, 0)
    per_batch = lambda b, i: (b, 0, 0)
    const = lambda b, i: (0, 0)
    grp = [pl.BlockSpec((1, tm // dil, dil * GROUP_WIDTH), row) for _, dil in ATTN_GROUPS]
    wide = pl.BlockSpec((1, tm, d), row)
    mod = pl.BlockSpec((1, 1, d), per_batch)
    full = lambda a: pl.BlockSpec(a.shape, const)
    w_ab, w_lb, w_out = (0.5 * w_ab).astype(BF16), (0.5 * w_lb).astype(BF16), w_out.astype(BF16)
    w_rt = w_router.T
    return pl.pallas_call(
        _merge_kernel,
        grid=(bsz, s // tm),
        in_specs=grp * 2 + [wide] * 4 + [mod, full(w_ab), full(w_lb), full(w_out),
                                          pl.BlockSpec((1, d), const), mod, mod, full(w_rt)],
        out_specs=[wide, wide, pl.BlockSpec((1, n_exp, tm), lambda b, i: (b, 0, i))],
        out_shape=[jax.ShapeDtypeStruct((bsz, s, d), F32), jax.ShapeDtypeStruct((bsz, s, d), BF16),
                   jax.ShapeDtypeStruct((bsz, n_exp, s), F32)],
        scratch_shapes=[pltpu.VMEM((SUB_TILES, GROUP_WIDTH // LANES, tm // SUB_TILES, LANES), F32)] * 4,
        compiler_params=_params("arbitrary", "arbitrary"),
        name="merge",
    )(*outs, *lses, lru, ga, gl, x, gate1.reshape(bsz, 1, d), w_ab, w_lb, w_out, norm_w.reshape(1, d),
      shift2.reshape(bsz, 1, d), scale2.reshape(bsz, 1, d), w_rt)


def _prefix_counts(flags, strict_upper):
    n_exp, s = flags.shape
    n_tiles = s // MXU_DIM
    stacked = jnp.concatenate([flags[:, k * MXU_DIM:(k + 1) * MXU_DIM] for k in range(n_tiles)], axis=0)
    within = jnp.dot(stacked.astype(BF16), strict_upper, preferred_element_type=F32)
    totals = jnp.sum(stacked, axis=1, keepdims=True)
    run = jnp.zeros((n_exp, 1), F32)
    pieces, bases = [], []
    for k in range(n_tiles):
        bases.append(run)
        pieces.append(within[k * n_exp:(k + 1) * n_exp] + run)
        run = run + totals[k * n_exp:(k + 1) * n_exp]
    bases.append(run)
    return jnp.concatenate(pieces, axis=1), bases


def _route_kernel(aff_ref, upper_ref, slot_ref, base_ref, *, capacity):
    aff = aff_ref[0]
    cap = jnp.float32(capacity)

    def refine(i, thr_bits):
        cand = thr_bits | jnp.left_shift(jnp.int32(1), 30 - i)
        cnt = jnp.sum(jnp.where(aff >= pltpu.bitcast(cand, F32), 1.0, 0.0), axis=1, keepdims=True)
        return jnp.where(cnt >= cap, cand, thr_bits)

    thr_bits = lax.fori_loop(0, 31, refine, jnp.zeros((aff.shape[0], 1), jnp.int32))
    thr = pltpu.bitcast(thr_bits, F32)
    above = jnp.where(aff > thr, 1.0, 0.0)
    tied = jnp.where(aff == thr, 1.0, 0.0)
    need = cap - jnp.sum(above, axis=1, keepdims=True)
    upper = upper_ref[...]
    tie_rank, _ = _prefix_counts(tied, upper)
    chosen = above + tied * jnp.where(tie_rank < need, 1.0, 0.0)
    slot, bases = _prefix_counts(chosen, upper)
    slot_ref[0] = jnp.where(chosen > 0.0, slot, -1.0).astype(jnp.int32)
    lane = lax.broadcasted_iota(jnp.int32, (aff.shape[0], LANES), 1)
    table = jnp.zeros((aff.shape[0], LANES), F32)
    for k, bk in enumerate(bases):
        table = jnp.where(lane == k, bk, table)
    base_ref[0] = table.astype(jnp.int32)


def _route(aff, capacity):
    bsz, n_exp, s = aff.shape
    idx = jnp.arange(MXU_DIM)
    upper = (idx[:, None] < idx[None, :]).astype(BF16)
    per_batch = lambda b: (b, 0, 0)
    return pl.pallas_call(
        functools.partial(_route_kernel, capacity=capacity),
        grid=(bsz,),
        in_specs=[pl.BlockSpec((1, n_exp, s), per_batch), pl.BlockSpec((MXU_DIM, MXU_DIM), lambda b: (0, 0))],
        out_specs=[pl.BlockSpec((1, n_exp, s), per_batch), pl.BlockSpec((1, n_exp, LANES), per_batch)],
        out_shape=[jax.ShapeDtypeStruct((bsz, n_exp, s), jnp.int32),
                   jax.ShapeDtypeStruct((bsz, n_exp, LANES), jnp.int32)],
        compiler_params=_params("arbitrary"),
        name="route",
    )(aff, upper)


def _window_plan(base_ref, row, chunk, capacity):
    first = base_ref[row + chunk]
    end = base_ref[row + chunk + 1]
    start = jnp.minimum((first // ROUTE_ALIGN) * ROUTE_ALIGN, capacity - ROUTE_WINDOW)
    n_windows = (end - start + ROUTE_WINDOW - 1) // ROUTE_WINDOW
    return pl.multiple_of(start, ROUTE_ALIGN), n_windows


def _later_window(start, k, capacity):
    lo = start + k * ROUTE_WINDOW
    return lo, pl.multiple_of(jnp.minimum(lo, capacity - ROUTE_WINDOW), ROUTE_ALIGN)


def _dispatch_kernel(base_ref, h_ref, slot_ref, aff_ref, xe_ref, gs_ref, *, n_exp, n_chunks, capacity):
    b, eg, c = pl.program_id(0), pl.program_id(1), pl.program_id(2)
    group = xe_ref.shape[0]
    chunk = h_ref.shape[1]

    @pl.when(c == 0)
    def _():
        xe_ref[...] = jnp.zeros_like(xe_ref)
        gs_ref[...] = jnp.zeros_like(gs_ref)

    h = h_ref[0]
    w_iota = lax.broadcasted_iota(jnp.int32, (ROUTE_WINDOW, chunk), 0)
    plans, hots = [], []
    for e in range(group):
        row = (b * n_exp + eg * group + e) * (n_chunks + 1)
        start, n_windows = _window_plan(base_ref, row, c, capacity)
        plans.append((start, n_windows))
        hots.append((slot_ref[0, e:e + 1, :] - start) == w_iota)
    stack = jnp.concatenate([jnp.where(hot, 1.0, 0.0) for hot in hots], axis=0).astype(BF16)
    rows = jnp.dot(stack, h, preferred_element_type=F32)
    most_windows = plans[0][1]
    for e in range(group):
        start, n_windows = plans[e]
        most_windows = jnp.maximum(most_windows, n_windows)
        win = pl.ds(start, ROUTE_WINDOW)
        xe_ref[e, 0, win, :] += rows[e * ROUTE_WINDOW:(e + 1) * ROUTE_WINDOW].astype(xe_ref.dtype)
        gs_ref[e, 0, win, :] += jnp.sum(jnp.where(hots[e], aff_ref[0, e:e + 1, :], 0.0), axis=1, keepdims=True)

    @pl.when(most_windows > 1)
    def _():
        for e in range(group):
            start, n_windows = plans[e]

            def more(k, carry, e=e, start=start):
                lo, st = _later_window(start, k, capacity)
                slots = slot_ref[0, e:e + 1, :]
                hot = jnp.logical_and(slots - st == w_iota, slots >= lo)
                extra = jnp.dot(jnp.where(hot, 1.0, 0.0).astype(BF16), h, preferred_element_type=F32)
                xe_ref[e, 0, pl.ds(st, ROUTE_WINDOW), :] += extra.astype(xe_ref.dtype)
                gs_ref[e, 0, pl.ds(st, ROUTE_WINDOW), :] += jnp.sum(
                    jnp.where(hot, aff_ref[0, e:e + 1, :], 0.0), axis=1, keepdims=True)
                return carry

            lax.fori_loop(1, n_windows, more, 0)


def _dispatch(h2, slot, aff, base_flat, capacity):
    bsz, s, d = h2.shape
    n_exp = slot.shape[1]
    chunk = min(ROUTE_CHUNK, s)
    group = min(EXPERT_GROUP, n_exp)
    n_chunks = s // chunk
    return pl.pallas_call(
        functools.partial(_dispatch_kernel, n_exp=n_exp, n_chunks=n_chunks, capacity=capacity),
        grid_spec=pltpu.PrefetchScalarGridSpec(
            num_scalar_prefetch=1,
            grid=(bsz, n_exp // group, n_chunks),
            in_specs=[pl.BlockSpec((1, chunk, d), lambda b, g, c, base: (b, c, 0)),
                      pl.BlockSpec((1, group, chunk), lambda b, g, c, base: (b, g, c)),
                      pl.BlockSpec((1, group, chunk), lambda b, g, c, base: (b, g, c))],
            out_specs=[pl.BlockSpec((group, 1, capacity, d), lambda b, g, c, base: (g, b, 0, 0)),
                       pl.BlockSpec((group, 1, capacity, 1), lambda b, g, c, base: (g, b, 0, 0))]),
        out_shape=[jax.ShapeDtypeStruct((n_exp, bsz, capacity, d), BF16),
                   jax.ShapeDtypeStruct((n_exp, bsz, capacity, 1), F32)],
        compiler_params=_params("arbitrary", "arbitrary", "arbitrary"),
        name="dispatch",
    )(base_flat, h2, slot, aff)


def _expert_kernel(x_ref, g_ref, wg_ref, wu_ref, wd_ref, y_ref):
    x = x_ref[0, 0]
    gate = jnp.dot(x, wg_ref[0], preferred_element_type=F32)
    up = jnp.dot(x, wu_ref[0], preferred_element_type=F32)
    he = (gate * _sigmoid(gate) * up).astype(BF16)
    y_ref[0, 0] = (jnp.dot(he, wd_ref[0], preferred_element_type=F32) * g_ref[0, 0]).astype(y_ref.dtype)


def _experts(xe, gates, w_gate, w_up, w_down):
    n_exp, bsz, cap, d = xe.shape
    ff = w_gate.shape[2]
    row = lambda e, i: (e, i, 0, 0)
    per_expert = lambda e, i: (e, 0, 0)
    return pl.pallas_call(
        _expert_kernel,
        grid=(n_exp, bsz),
        in_specs=[pl.BlockSpec((1, 1, cap, d), row), pl.BlockSpec((1, 1, cap, 1), row),
                  pl.BlockSpec((1, d, ff), per_expert), pl.BlockSpec((1, d, ff), per_expert),
                  pl.BlockSpec((1, ff, d), per_expert)],
        out_specs=pl.BlockSpec((1, 1, cap, d), row),
        out_shape=jax.ShapeDtypeStruct(xe.shape, BF16),
        compiler_params=_params("arbitrary", "arbitrary"),
        name="experts",
    )(xe, gates, w_gate.astype(BF16), w_up.astype(BF16), w_down.astype(BF16))


def _combine_kernel(base_ref, y_ref, slot_ref, x1_ref, gate_ref, out_ref, acc_s, *, n_chunks, capacity):
    b, c = pl.program_id(0), pl.program_id(1)
    n_exp = y_ref.shape[0]
    chunk = x1_ref.shape[1]
    w_iota = lax.broadcasted_iota(jnp.int32, (ROUTE_WINDOW, chunk), 0)
    plans, hots, wins = [], [], []
    for e in range(n_exp):
        start, n_windows = _window_plan(base_ref, (b * n_exp + e) * (n_chunks + 1), c, capacity)
        plans.append((start, n_windows))
        hots.append(jnp.where((slot_ref[0, e:e + 1, :] - start) == w_iota, 1.0, 0.0))
        wins.append(y_ref[e, 0, pl.ds(start, ROUTE_WINDOW), :])
    hot = jnp.concatenate(hots, axis=0).astype(BF16)
    ywin = jnp.concatenate(wins, axis=0)
    acc = lax.dot_general(hot, ywin, (((0,), (0,)), ((), ())), preferred_element_type=F32)
    most_windows = plans[0][1]
    for _, n_windows in plans[1:]:
        most_windows = jnp.maximum(most_windows, n_windows)

    @pl.when(most_windows <= 1)
    def _():
        out_ref[0] = x1_ref[0] + gate_ref[0] * acc

    @pl.when(most_windows > 1)
    def _():
        acc_s[...] = acc
        for e in range(n_exp):
            start, n_windows = plans[e]

            def more(k, carry, e=e, start=start):
                lo, st = _later_window(start, k, capacity)
                slots = slot_ref[0, e:e + 1, :]
                sel = jnp.logical_and(slots - st == w_iota, slots >= lo)
                acc_s[...] += lax.dot_general(jnp.where(sel, 1.0, 0.0).astype(BF16),
                                              y_ref[e, 0, pl.ds(st, ROUTE_WINDOW), :],
                                              (((0,), (0,)), ((), ())), preferred_element_type=F32)
                return carry

            lax.fori_loop(1, n_windows, more, 0)
        out_ref[0] = x1_ref[0] + gate_ref[0] * acc_s[...]


def _combine(ys, slot, base_flat, x1, gate2, capacity):
    bsz, s, d = x1.shape
    n_exp = ys.shape[0]
    chunk = min(ROUTE_CHUNK, s)
    n_chunks = s // chunk
    return pl.pallas_call(
        functools.partial(_combine_kernel, n_chunks=n_chunks, capacity=capacity),
        grid_spec=pltpu.PrefetchScalarGridSpec(
            num_scalar_prefetch=1,
            grid=(bsz, n_chunks),
            in_specs=[pl.BlockSpec((n_exp, 1, capacity, d), lambda b, c, base: (0, b, 0, 0)),
                      pl.BlockSpec((1, n_exp, chunk), lambda b, c, base: (b, 0, c)),
                      pl.BlockSpec((1, chunk, d), lambda b, c, base: (b, c, 0)),
                      pl.BlockSpec((1, 1, d), lambda b, c, base: (b, 0, 0))],
            out_specs=pl.BlockSpec((1, chunk, d), lambda b, c, base: (b, c, 0)),
            scratch_shapes=[pltpu.VMEM((chunk, d), F32)]),
        out_shape=jax.ShapeDtypeStruct(x1.shape, F32),
        compiler_params=_params("arbitrary", "arbitrary"),
        name="combine",
    )(base_flat, ys, slot, x1, gate2.reshape(bsz, 1, d))


def _layer(x, c, positions, w_ada, b_ada, norm1_w, w_in, q_norm_w, k_norm_w, conv_w, conv_b, lru_w_a, lru_b_a,
           lru_w_x, lru_b_x, lru_lambda, w_attn_branch, w_lru_branch, w_out, norm2_w, w_router, w_gate, w_up,
           w_down):
    bsz, s, d = x.shape
    mod = _ada(c, w_ada, b_ada)
    shift1, scale1, gate1, shift2, scale2, gate2 = jnp.split(mod, 6, axis=-1)

    (q0, q1, q2, k0, k1, k2, v0, v1, v2, xr, yr, ga, gl) = _inproj(x, positions, shift1, scale1, norm1_w, w_in,
                                                                  q_norm_w, k_norm_w)

    outs, lses = [], []
    for (window, dilation), q, k, v in zip(ATTN_GROUPS, (q0, q1, q2), (k0, k1, k2), (v0, v1, v2)):
        o, lse = _attention_group(q, k, v, window, dilation)
        outs.append(o)
        lses.append(lse)

    h_fwd = _lru_pass(xr, conv_w, conv_b, lru_w_a[0], lru_b_a[0], lru_w_x[0], lru_b_x[0], lru_lambda[0],
                      reverse=False)
    lru = _lru_pass(xr, conv_w, conv_b, lru_w_a[1], lru_b_a[1], lru_w_x[1], lru_b_x[1], lru_lambda[1],
                    reverse=True, h_fwd=h_fwd, yr=yr)

    x1, h2, aff = _merge(outs, lses, lru, ga, gl, x, gate1, w_attn_branch, w_lru_branch, w_out, norm2_w,
                         shift2, scale2, w_router)

    capacity = max(1, CAPACITY_FACTOR * s // N_EXPERTS)
    n_chunks = s // min(ROUTE_CHUNK, s)
    slot, base_table = _route(aff, capacity)
    base_flat = base_table[:, :, :n_chunks + 1].reshape(-1)
    xe, gates = _dispatch(h2, slot, aff, base_flat, capacity)
    ys = _experts(xe, gates, w_gate, w_up, w_down)
    return _combine(ys, slot, base_flat, x1, gate2, capacity)


def kernel(x, c, positions, w_ada, b_ada, norm1_w, w_in, q_norm_w, k_norm_w, conv_w, conv_b, lru_w_a, lru_b_a,
           lru_w_x, lru_b_x, lru_lambda, w_attn_branch, w_lru_branch, w_out, norm2_w, w_router, w_gate, w_up,
           w_down):
    for l in range(w_ada.shape[0]):
        x = _layer(x, c, positions, w_ada[l], b_ada[l], norm1_w[l], w_in[l], q_norm_w[l], k_norm_w[l],
                   conv_w[l, :, 0, :], conv_b[l], lru_w_a[l], lru_b_a[l], lru_w_x[l], lru_b_x[l], lru_lambda[l],
                   w_attn_branch[l], w_lru_branch[l], w_out[l], norm2_w[l], w_router[l], w_gate[l], w_up[l],
                   w_down[l])
    return x
```

```python
import functools

import jax
import jax.numpy as jnp
from jax import lax
from jax.experimental import pallas as pl
from jax.experimental.pallas import tpu as pltpu

F32 = jnp.float32
BF16 = jnp.bfloat16

HEAD_DIM = 64
HEADS_PER_GROUP = 4
GROUP_WIDTH = HEADS_PER_GROUP * HEAD_DIM
ATTN_GROUPS = ((128, 1), (512, 4), (2048, 16))
ATTN_WIDTH = GROUP_WIDTH * len(ATTN_GROUPS)
LRU_BLOCKS = 16
LRU_C = 8.0
CONV_WIDTH = 4
N_EXPERTS = 16
CAPACITY_FACTOR = 2
ROPE_THETA = 10000.0
NORM_EPS = 1e-6
NEG_INF = -1e30
LOG2_E = 1.4426950408889634
TINY = 1e-30

LANES = 128
SUBLANES = 8
MXU_DIM = 256
VMEM_LIMIT = 56 * 1024 * 1024

ROW_TILE = 512
SUB_TILES = 1
WIDE_ROW_TILE = 1024
ATTN_TQ = 128
ATTN_CLASSES_PER_STEP = 4
LRU_TC = 128
LRU_NB = 8
HALO = 16
ROUTE_CHUNK = 256
ROUTE_WINDOW = 64
ROUTE_ALIGN = 16
EXPERT_GROUP = 16
ROPE_PACK = LANES // (HEAD_DIM // 2)


def _sigmoid(t):
    return 0.5 * jnp.tanh(0.5 * t) + 0.5


def _params(*sem):
    return pltpu.CompilerParams(dimension_semantics=sem, vmem_limit_bytes=VMEM_LIMIT)


def _ada_kernel(c_ref, w_ref, b_ref, o_ref):
    c = c_ref[...]
    o_ref[...] = jnp.dot(c * _sigmoid(c), w_ref[...], preferred_element_type=F32,
                         precision=lax.Precision.HIGHEST) + b_ref[...]


def _ada(c, w, b):
    bsz, d = c.shape
    n = w.shape[1]
    tn = n // 4
    return pl.pallas_call(
        _ada_kernel,
        grid=(n // tn,),
        in_specs=[pl.BlockSpec((bsz, d), lambda j: (0, 0)),
                  pl.BlockSpec((d, tn), lambda j: (0, j)),
                  pl.BlockSpec((1, tn), lambda j: (0, j))],
        out_specs=pl.BlockSpec((bsz, tn), lambda j: (0, j)),
        out_shape=jax.ShapeDtypeStruct((bsz, n), F32),
        compiler_params=_params("arbitrary"),
        name="ada",
    )(c, w, b.reshape(1, n))


def _store_classes(val, out_ref, row0, relayout_s, dilation):
    n = val.shape[0]
    if dilation == 1:
        out_ref[0, row0:row0 + n, :] = val.astype(out_ref.dtype)
        return
    n_tiles = GROUP_WIDTH // LANES
    rows = n // dilation
    for t in range(n_tiles):
        relayout_s[t] = val[:, t * LANES:(t + 1) * LANES]
    for r in range(dilation):
        for t in range(n_tiles):
            lo = r * GROUP_WIDTH + t * LANES
            out_ref[0, row0 // dilation:row0 // dilation + rows, lo:lo + LANES] = (
                relayout_s[t, pl.ds(r, rows, stride=dilation), :].astype(out_ref.dtype))


def _inproj_kernel(x_ref, pos_ref, shift_ref, scale_ref, nw_ref, w_ref, qnw_ref, knw_ref, invf_ref, seg_ref,
                   spread_ref, q0_ref, q1_ref, q2_ref, k0_ref, k1_ref, k2_ref, v0_ref, v1_ref, v2_ref,
                   relayout_s, cos_s, sin_s):
    tm, d = x_ref.shape[1:]
    sub = tm // SUB_TILES
    lane = lax.broadcasted_iota(jnp.int32, (sub, GROUP_WIDTH), 1)
    first_half = (lane & (HEAD_DIM // 2)) == 0
    seg = seg_ref[...]

    for part in range(SUB_TILES):
        row0 = part * sub
        x = x_ref[0, row0:row0 + sub, :]
        ms = jnp.mean(x * x, axis=-1, keepdims=True)
        h = x * lax.rsqrt(ms + NORM_EPS) * nw_ref[...]
        hb = (h * (1.0 + scale_ref[0]) + shift_ref[0]).astype(BF16)

        packed_rows = sub // ROPE_PACK
        ang = pos_ref[0, row0 // ROPE_PACK:row0 // ROPE_PACK + packed_rows, :].astype(F32) * invf_ref[...]
        cos_p = jnp.cos(ang)
        sin_p = jnp.sin(ang)
        for p in range(ROPE_PACK):
            rows_p = pl.ds(p, packed_rows, stride=ROPE_PACK)
            cos_s[rows_p, :] = jnp.dot(cos_p, spread_ref[p], preferred_element_type=F32,
                                       precision=lax.Precision.HIGHEST)
            sin_s[rows_p, :] = jnp.dot(sin_p, spread_ref[p], preferred_element_type=F32,
                                       precision=lax.Precision.HIGHEST)
        cos = cos_s[...]
        sin = sin_s[...]
        cos2 = jnp.concatenate([cos, cos], axis=1)
        sin2 = jnp.concatenate([sin, sin], axis=1)
        sin2 = jnp.where(first_half, -sin2, sin2)
        scratch = relayout_s.at[part]

        def project(off, j):
            lo_c = off + j * GROUP_WIDTH
            t = jnp.dot(hb, w_ref[:, lo_c:lo_c + GROUP_WIDTH], preferred_element_type=F32)
            return t, (t * t).astype(BF16)

        def norm_rope(projected, w_norm_ref, out_ref, j):
            t, squares = projected
            lo_c = j * GROUP_WIDTH
            msq = jnp.dot(squares, seg, preferred_element_type=F32) * (1.0 / HEAD_DIM)
            y = t * lax.rsqrt(msq + NORM_EPS) * w_norm_ref[:, lo_c:lo_c + GROUP_WIDTH]
            rot = jnp.where(first_half, pltpu.roll(y, GROUP_WIDTH - HEAD_DIM // 2, 1),
                            pltpu.roll(y, HEAD_DIM // 2, 1))
            _store_classes(y * cos2 + rot * sin2, out_ref, row0, scratch, ATTN_GROUPS[j][1])

        def value(out_ref, j):
            off = 2 * ATTN_WIDTH + j * GROUP_WIDTH
            v = jnp.dot(hb, w_ref[:, off:off + GROUP_WIDTH], preferred_element_type=F32)
            _store_classes(v, out_ref, row0, scratch, ATTN_GROUPS[j][1])

        tiles = [(0, qnw_ref, q0_ref, 0), (ATTN_WIDTH, knw_ref, k0_ref, 0), (0, qnw_ref, q1_ref, 1),
                 (ATTN_WIDTH, knw_ref, k1_ref, 1), (0, qnw_ref, q2_ref, 2), (ATTN_WIDTH, knw_ref, k2_ref, 2)]
        values = [(v0_ref, 0), (v1_ref, 1), (v2_ref, 2)]
        ahead = project(tiles[0][0], tiles[0][3])
        for n, (off, w_norm_ref, out_ref, j) in enumerate(tiles):
            current = ahead
            if n + 1 < len(tiles):
                ahead = project(tiles[n + 1][0], tiles[n + 1][3])
            if n < len(values):
                value(*values[n])
            norm_rope(current, w_norm_ref, out_ref, j)


def _inproj_wide_kernel(x_ref, shift_ref, scale_ref, nw_ref, w_ref, *out_refs):
    x = x_ref[0]
    d = x.shape[1]
    ms = jnp.mean(x * x, axis=-1, keepdims=True)
    h = x * lax.rsqrt(ms + NORM_EPS) * nw_ref[...]
    hb = (h * (1.0 + scale_ref[0]) + shift_ref[0]).astype(BF16)
    for i, out_ref in enumerate(out_refs):
        out_ref[0] = jnp.dot(hb, w_ref[:, i * d:(i + 1) * d], preferred_element_type=F32).astype(out_ref.dtype)


def _inproj_wide(x, shift, scale, norm_w, w_wide):
    bsz, s, d = x.shape
    n_out = w_wide.shape[1] // d
    tm = min(WIDE_ROW_TILE, s)
    row = lambda b, i: (b, i, 0)
    per_batch = lambda b, i: (b, 0, 0)
    const = lambda b, i: (0, 0)
    return pl.pallas_call(
        _inproj_wide_kernel,
        grid=(bsz, s // tm),
        in_specs=[pl.BlockSpec((1, tm, d), row),
                  pl.BlockSpec((1, 1, d), per_batch),
                  pl.BlockSpec((1, 1, d), per_batch),
                  pl.BlockSpec((1, d), const),
                  pl.BlockSpec(w_wide.shape, const, pipeline_mode=pl.Buffered(1))],
        out_specs=[pl.BlockSpec((1, tm, d), row)] * n_out,
        out_shape=[jax.ShapeDtypeStruct((bsz, s, d), BF16)] * n_out,
        compiler_params=_params("arbitrary", "arbitrary"),
        name="inproj_wide",
    )(x, shift.reshape(bsz, 1, d), scale.reshape(bsz, 1, d), norm_w.reshape(1, d), w_wide)


def _inproj(x, positions, shift, scale, norm_w, w_in, q_norm_w, k_norm_w):
    bsz, s, d = x.shape
    tm = min(ROW_TILE, s)
    half = HEAD_DIM // 2
    inv_freq = ROPE_THETA ** (-jnp.arange(half, dtype=F32) / half)
    inv_freq = jnp.tile(inv_freq, ROPE_PACK).reshape(1, LANES)
    pos_packed = jnp.repeat(positions.reshape(bsz, s // ROPE_PACK, ROPE_PACK), half, axis=-1)
    lane = jnp.arange(LANES)
    spread = (lane[None, :, None] == (jnp.arange(ROPE_PACK)[:, None, None] * half + lane[None, None, :] % half))
    spread = spread.astype(F32)
    head_of = jnp.arange(GROUP_WIDTH) // HEAD_DIM
    seg = (head_of[:, None] == head_of[None, :]).astype(BF16)
    row = lambda b, i: (b, i, 0)
    per_batch = lambda b, i: (b, 0, 0)
    const = lambda b, i: (0, 0)
    narrow = [jax.ShapeDtypeStruct((bsz, s // dil, dil * GROUP_WIDTH), BF16) for _, dil in ATTN_GROUPS] * 3
    narrow_specs = [pl.BlockSpec((1, tm // dil, dil * GROUP_WIDTH), row) for _, dil in ATTN_GROUPS] * 3
    return pl.pallas_call(
        _inproj_kernel,
        grid=(bsz, s // tm),
        in_specs=[pl.BlockSpec((1, tm, d), row),
                  pl.BlockSpec((1, tm // ROPE_PACK, LANES), row),
                  pl.BlockSpec((1, 1, d), per_batch),
                  pl.BlockSpec((1, 1, d), per_batch),
                  pl.BlockSpec((1, d), const),
                  pl.BlockSpec(w_in.shape, const),
                  pl.BlockSpec((1, ATTN_WIDTH), const),
                  pl.BlockSpec((1, ATTN_WIDTH), const),
                  pl.BlockSpec((1, LANES), const),
                  pl.BlockSpec((GROUP_WIDTH, GROUP_WIDTH), const),
                  pl.BlockSpec(spread.shape, lambda b, i: (0, 0, 0))],
        out_specs=narrow_specs,
        out_shape=narrow,
        scratch_shapes=[pltpu.VMEM((SUB_TILES, GROUP_WIDTH // LANES, tm // SUB_TILES, LANES), F32),
                        pltpu.VMEM((tm // SUB_TILES, LANES), F32), pltpu.VMEM((tm // SUB_TILES, LANES), F32)],
        compiler_params=_params("arbitrary", "arbitrary"),
        name="inproj",
    )(x, pos_packed, shift.reshape(bsz, 1, d), scale.reshape(bsz, 1, d),
      norm_w.reshape(1, d), w_in, (q_norm_w * HEAD_DIM ** -0.5).reshape(1, ATTN_WIDTH),
      k_norm_w.reshape(1, ATTN_WIDTH), inv_freq, seg, spread)


def _attn_kernel(q_ref, k_ref, v_ref, o_ref, lse_ref, *, length, tq, tk, half):
    rel0 = (lax.broadcasted_iota(jnp.int32, (2 * tq, tk), 1)
            - (lax.broadcasted_iota(jnp.int32, (2 * tq, tk), 0) & (tq - 1)))
    lane = lax.broadcasted_iota(jnp.int32, (tq, LANES), 1)
    head_a = lane < HEAD_DIM
    n_tiles = length // tq

    per_trip = 4 if n_tiles % 4 == 0 else (2 if n_tiles % 2 == 0 else 1)

    def scores(item):
        t0, ws, _, cols = item
        qp = q_ref[0, pl.ds(t0, tq), cols]
        kp = k_ref[0, pl.ds(ws, tk), cols]
        zero = jnp.zeros_like(qp)
        q2 = jnp.concatenate([jnp.where(head_a, qp, zero), jnp.where(head_a, zero, qp)], axis=0)
        return lax.dot_general(q2, kp, (((1,), (1,)), ((), ())), preferred_element_type=F32)

    def trip(g, carry, col0):
        items = []
        for u in range(per_trip):
            t0 = pl.multiple_of((g * per_trip + u) * tq, tq)
            ws = pl.multiple_of(jnp.clip(t0 - half, 0, length - tk), half)
            valid = jnp.abs(rel0 + (ws - t0)) <= half
            for p in range(GROUP_WIDTH // LANES):
                items.append((t0, ws, valid, slice(col0 + p * LANES, col0 + (p + 1) * LANES)))
        ahead = scores(items[0])
        for n, (t0, ws, valid, cols) in enumerate(items):
            s = ahead
            if n + 1 < len(items):
                ahead = scores(items[n + 1])
            s = jnp.where(valid, s, NEG_INF)
            m = jnp.max(s, axis=1, keepdims=True)
            e = jnp.exp(s - m)
            den = jnp.sum(e, axis=1, keepdims=True)
            vp = v_ref[0, pl.ds(ws, tk), cols]
            o = jnp.dot(e.astype(BF16), vp, preferred_element_type=F32) * (1.0 / den)
            lse = jnp.broadcast_to(m + jnp.log(den), (2 * tq, LANES))
            o_ref[0, pl.ds(t0, tq), cols] = jnp.where(head_a, o[:tq], o[tq:]).astype(o_ref.dtype)
            lse_ref[0, pl.ds(t0, tq), cols] = jnp.where(head_a, lse[:tq], lse[tq:])
        return carry

    for cls in range(q_ref.shape[2] // GROUP_WIDTH):
        lax.fori_loop(0, n_tiles // per_trip, functools.partial(trip, col0=cls * GROUP_WIDTH), 0)


def _attention_group(q, k, v, window, dilation):
    bsz, length, _ = q.shape
    half = window // (2 * dilation)
    tq = min(ATTN_TQ, length)
    assert tq & (tq - 1) == 0, "query tile must be a power of two"
    tk = min(tq + 2 * half, length)
    classes = min(dilation, ATTN_CLASSES_PER_STEP)
    spec = pl.BlockSpec((1, length, classes * GROUP_WIDTH), lambda b, r: (b, 0, r))
    return pl.pallas_call(
        functools.partial(_attn_kernel, length=length, tq=tq, tk=tk, half=half),
        grid=(bsz, dilation // classes),
        in_specs=[spec] * 3,
        out_specs=[spec] * 2,
        out_shape=[jax.ShapeDtypeStruct(q.shape, BF16), jax.ShapeDtypeStruct(q.shape, F32)],
        compiler_params=_params("arbitrary", "arbitrary"),
        name=f"attn_d{dilation}",
    )(q, k, v)


def _gelu_tanh(t):
    return 0.5 * t * (1.0 + jnp.tanh(0.7978845608028654 * (t + 0.044715 * t * t * t)))


def _lru_kernel(*refs, reverse, final, n_chunks):
    if final:
        (xr_ref, xp_ref, xn_ref, hf_ref, yr_ref, shift_ref, cw_ref, cb_ref, wa_ref, wx_ref, ba_ref, bx_ref, lam_ref,
         out_ref, a_s, u_s, h_s, state_s) = refs
    else:
        (xr_ref, xp_ref, xn_ref, shift_ref, cw_ref, cb_ref, wa_ref, wx_ref, ba_ref, bx_ref, lam_ref,
         out_ref, a_s, u_s, h_s, state_s) = refs
    nb, tc, ch = xr_ref.shape
    n_slabs = ch // LANES
    step = pl.program_id(1)
    chunk = (n_chunks - 1 - step) if reverse else step

    @pl.when(step == 0)
    def _():
        state_s[...] = jnp.zeros_like(state_s)

    lam = lam_ref[...]
    neg_c_softplus = -LRU_C * (jnp.maximum(-lam, 0.0) + jnp.log(1.0 + jnp.exp(-jnp.abs(lam))))
    half_log2_decay = (0.5 * LOG2_E) * neg_c_softplus
    pad_rows = jnp.zeros((shift_ref.shape[1] - tc - 2 * HALO, ch), BF16)

    rows_per_trip = 2 if nb % 2 == 0 else 1
    n_tiles = ch // MXU_DIM
    tile_cols = [slice(kk * MXU_DIM, (kk + 1) * MXU_DIM) for kk in range(n_tiles)]
    half_cw = 0.5 * cw_ref[...]
    half_cb = 0.5 * cb_ref[...]

    def coeffs(trip, carry):
        batch_rows = [trip * rows_per_trip + r for r in range(rows_per_trip)]
        xcs = []
        for b in batch_rows:
            before = jnp.where(chunk > 0, xp_ref[b], jnp.zeros((HALO, ch), BF16))
            after = jnp.where(chunk < n_chunks - 1, xn_ref[b], jnp.zeros((HALO, ch), BF16))
            stack = jnp.concatenate([before, xr_ref[b], after, pad_rows], axis=0)
            for cols in tile_cols:
                taps = jnp.dot(shift_ref[...], stack[:, cols], preferred_element_type=F32)
                half_x = half_cb[:, cols]
                for j in range(CONV_WIDTH):
                    half_x = half_x + half_cw[j:j + 1, cols] * taps[j * tc:(j + 1) * tc]
                xcs.append(half_x)
        pre = []
        for n, half_x in enumerate(xcs):
            kk = n % n_tiles
            xcb = half_x.astype(BF16)
            pre.append((jnp.dot(xcb, wa_ref[kk], preferred_element_type=F32),
                        jnp.dot(xcb, wx_ref[kk], preferred_element_type=F32)))
        for n, half_x in enumerate(xcs):
            kk, cols = n % n_tiles, tile_cols[n % n_tiles]
            slab_rows = pl.ds(batch_rows[n // n_tiles], tc, stride=nb)
            tanh_r = jnp.tanh(pre[n][0] + ba_ref[:, cols])
            tanh_i = jnp.tanh(pre[n][1] + bx_ref[:, cols])
            a = jnp.exp2(tanh_r * half_log2_decay[:, cols] + half_log2_decay[:, cols])
            v = 1.0 - a * a
            root = v * lax.rsqrt(jnp.maximum(v, TINY))
            u = root * (tanh_i * half_x + half_x)
            for t in range(MXU_DIM // LANES):
                j = kk * (MXU_DIM // LANES) + t
                a_s[j, slab_rows, :] = a[:, t * LANES:(t + 1) * LANES]
                u_s[j, slab_rows, :] = u[:, t * LANES:(t + 1) * LANES]
        return carry

    lax.fori_loop(0, nb // rows_per_trip, coeffs, 0)

    def scan_step(i, hs):
        t = (tc - 1 - i) if reverse else i
        new = []
        for j in range(n_slabs):
            rows = pl.ds(pl.multiple_of(t * nb, nb), nb)
            hj = a_s[j, rows, :] * hs[j] + u_s[j, rows, :]
            h_s[j, rows, :] = hj
            new.append(hj)
        return tuple(new)

    hs = lax.fori_loop(0, tc, scan_step, tuple(state_s[j] for j in range(n_slabs)), unroll=2)
    for j in range(n_slabs):
        state_s[j] = hs[j]

    def emit(b, carry):
        hb = jnp.concatenate([h_s[j, pl.ds(b, tc, stride=nb), :] for j in range(n_slabs)], axis=1)
        if final:
            hb = (hb + hf_ref[b].astype(F32)) * _gelu_tanh(yr_ref[b].astype(F32))
        out_ref[b] = hb.astype(out_ref.dtype)
        return carry

    lax.fori_loop(0, nb, emit, 0, unroll=2 if nb % 2 == 0 else 1)


def _block_diag_tiles(w):
    per_tile = MXU_DIM // w.shape[1]
    n_tiles = w.shape[0] // per_tile
    w = w.reshape(n_tiles, per_tile, w.shape[1], w.shape[2])
    eye = jnp.eye(per_tile, dtype=w.dtype)
    tiles = jnp.einsum('tpkj,pq->tpkqj', w, eye)
    return tiles.reshape(n_tiles, MXU_DIM, MXU_DIM).astype(BF16)


def _lru_pass(xr, conv_w, conv_b, w_a, b_a, w_x, b_x, lam, *, reverse, h_fwd=None, yr=None):
    bsz, s, ch = xr.shape
    nb = min(LRU_NB, bsz)
    tc = min(LRU_TC, s)
    n_chunks = s // tc
    final = h_fwd is not None
    chunk_of = (lambda c: n_chunks - 1 - c) if reverse else (lambda c: c)
    per_halo = tc // HALO
    last_halo = s // HALO - 1
    main = pl.BlockSpec((nb, tc, ch), lambda g, c: (g, chunk_of(c), 0))
    prev = pl.BlockSpec((nb, HALO, ch), lambda g, c: (g, jnp.maximum(chunk_of(c) * per_halo - 1, 0), 0))
    nxt = pl.BlockSpec((nb, HALO, ch), lambda g, c: (g, jnp.minimum((chunk_of(c) + 1) * per_halo, last_halo), 0))
    const2 = lambda g, c: (0, 0)
    const3 = lambda g, c: (0, 0, 0)
    vec = pl.BlockSpec((1, ch), const2)
    tiles = pl.BlockSpec((ch // MXU_DIM, MXU_DIM, MXU_DIM), const3)
    acts = [xr, xr, xr] + ([h_fwd, yr] if final else [])
    act_specs = [main, prev, nxt] + ([main, main] if final else [])
    slab = pltpu.VMEM((ch // LANES, tc * nb, LANES), F32)
    stack_rows = -(-(tc + 2 * HALO) // MXU_DIM) * MXU_DIM
    tap_row = HALO - CONV_WIDTH // 2 + jnp.arange(CONV_WIDTH)[:, None] + jnp.arange(tc)[None, :]
    shifts = (tap_row.reshape(-1, 1) == jnp.arange(stack_rows)[None, :]).astype(BF16)
    return pl.pallas_call(
        functools.partial(_lru_kernel, reverse=reverse, final=final, n_chunks=n_chunks),
        grid=(bsz // nb, n_chunks),
        in_specs=act_specs + [pl.BlockSpec(shifts.shape, const2), pl.BlockSpec((CONV_WIDTH, ch), const2), vec,
                              tiles, tiles, vec, vec, vec],
        out_specs=main,
        out_shape=jax.ShapeDtypeStruct((bsz, s, ch), BF16),
        scratch_shapes=[slab, slab, slab, pltpu.VMEM((ch // LANES, nb, LANES), F32)],
        compiler_params=_params("arbitrary", "arbitrary"),
        name="lru_bwd" if reverse else "lru_fwd",
    )(*acts, shifts, conv_w.reshape(CONV_WIDTH, ch), conv_b.reshape(1, ch), _block_diag_tiles(w_a),
      _block_diag_tiles(w_x), (0.5 * b_a).reshape(1, ch), (0.5 * b_x).reshape(1, ch), lam.reshape(1, ch))


def _load_classes(in_ref, row0, n, relayout_s, dilation):
    if dilation == 1:
        return in_ref[0, row0:row0 + n, :].astype(F32)
    n_tiles = GROUP_WIDTH // LANES
    rows = n // dilation
    for r in range(dilation):
        for t in range(n_tiles):
            lo = r * GROUP_WIDTH + t * LANES
            relayout_s[t, pl.ds(r, rows, stride=dilation), :] = (
                in_ref[0, row0 // dilation:row0 // dilation + rows, lo:lo + LANES].astype(F32))
    return jnp.concatenate([relayout_s[t] for t in range(n_tiles)], axis=1)


def _merge_kernel(o0_ref, o1_ref, o2_ref, l0_ref, l1_ref, l2_ref, lru_ref, ga_ref, gl_ref, x_ref, gate_ref,
                  wab_ref, wlb_ref, wout_ref, nw_ref, shift_ref, scale_ref, wr_ref,
                  x1_ref, h2_ref, aff_ref, so1_s, so2_s, sl1_s, sl2_s):
    dil = [dilation for _, dilation in ATTN_GROUPS]
    sub = x_ref.shape[1] // SUB_TILES
    for part in range(SUB_TILES):
        row0 = part * sub
        rows = slice(row0, row0 + sub)
        l0 = _load_classes(l0_ref, row0, sub, None, dil[0])
        l1 = _load_classes(l1_ref, row0, sub, sl1_s.at[part], dil[1])
        l2 = _load_classes(l2_ref, row0, sub, sl2_s.at[part], dil[2])
        o0 = _load_classes(o0_ref, row0, sub, None, dil[0])
        o1 = _load_classes(o1_ref, row0, sub, so1_s.at[part], dil[1])
        o2 = _load_classes(o2_ref, row0, sub, so2_s.at[part], dil[2])
        m = jnp.maximum(jnp.maximum(l0, l1), l2)
        e0, e1, e2 = jnp.exp(l0 - m), jnp.exp(l1 - m), jnp.exp(l2 - m)
        attn = (e0 * o0 + e1 * o1 + e2 * o2) / (e0 + e1 + e2)
        branch_a = jnp.dot(attn.astype(BF16), wab_ref[...], preferred_element_type=F32)
        branch_l = jnp.dot(lru_ref[0, rows, :], wlb_ref[...], preferred_element_type=F32)
        merged = ((jnp.tanh(ga_ref[0, rows, :].astype(F32)) + 1.0) * branch_a
                  + (jnp.tanh(gl_ref[0, rows, :].astype(F32)) + 1.0) * branch_l)
        mix = jnp.dot(merged.astype(BF16), wout_ref[...], preferred_element_type=F32)
        x1 = x_ref[0, rows, :] + gate_ref[0] * mix
        x1_ref[0, rows, :] = x1
        ms = jnp.mean(x1 * x1, axis=-1, keepdims=True)
        h2 = x1 * lax.rsqrt(ms + NORM_EPS) * nw_ref[...]
        h2 = h2 * (1.0 + scale_ref[0]) + shift_ref[0]
        h2_ref[0, rows, :] = h2.astype(h2_ref.dtype)
        logits = lax.dot_general(wr_ref[...], h2, (((1,), (1,)), ((), ())), preferred_element_type=F32,
                                 precision=lax.Precision.HIGHEST)
        z = jnp.exp(logits - jnp.max(logits, axis=0, keepdims=True))
        aff_ref[0, :, rows] = z / jnp.sum(z, axis=0, keepdims=True)


def _merge(outs, lses, lru, ga, gl, x, gate1, w_ab, w_lb, w_out, norm_w, shift2, scale2, w_router):
    bsz, s, d = x.shape
    tm = min(ROW_TILE, s)
    n_exp = w_router.shape[1]
    row = lambda b, i: (b, i, 0)
    per_batch = lambda b, i: (b, 0, 0)
    const = lambda b, i: (0, 0)
    grp = [pl.BlockSpec((1, tm // dil, dil * GROUP_WIDTH), row) for _, dil in ATTN_GROUPS]
    wide = pl.BlockSpec((1, tm, d), row)
    mod = pl.BlockSpec((1, 1, d), per_batch)
    full = lambda a: pl.BlockSpec(a.shape, const)
    w_ab, w_lb, w_out = (0.5 * w_ab).astype(BF16), (0.5 * w_lb).astype(BF16), w_out.astype(BF16)
    w_rt = w_router.T
    return pl.pallas_call(
        _merge_kernel,
        grid=(bsz, s // tm),
        in_specs=grp * 2 + [wide] * 4 + [mod, full(w_ab), full(w_lb), full(w_out),
                                          pl.BlockSpec((1, d), const), mod, mod, full(w_rt)],
        out_specs=[wide, wide, pl.BlockSpec((1, n_exp, tm), lambda b, i: (b, 0, i))],
        out_shape=[jax.ShapeDtypeStruct((bsz, s, d), F32), jax.ShapeDtypeStruct((bsz, s, d), BF16),
                   jax.ShapeDtypeStruct((bsz, n_exp, s), F32)],
        scratch_shapes=[pltpu.VMEM((SUB_TILES, GROUP_WIDTH // LANES, tm // SUB_TILES, LANES), F32)] * 4,
        compiler_params=_params("arbitrary", "arbitrary"),
        name="merge",
    )(*outs, *lses, lru, ga, gl, x, gate1.reshape(bsz, 1, d), w_ab, w_lb, w_out, norm_w.reshape(1, d),
      shift2.reshape(bsz, 1, d), scale2.reshape(bsz, 1, d), w_rt)


def _prefix_counts(flags, strict_upper):
    n_exp, s = flags.shape
    n_tiles = s // MXU_DIM
    stacked = jnp.concatenate([flags[:, k * MXU_DIM:(k + 1) * MXU_DIM] for k in range(n_tiles)], axis=0)
    within = jnp.dot(stacked.astype(BF16), strict_upper, preferred_element_type=F32)
    totals = jnp.sum(stacked, axis=1, keepdims=True)
    run = jnp.zeros((n_exp, 1), F32)
    pieces, bases = [], []
    for k in range(n_tiles):
        bases.append(run)
        pieces.append(within[k * n_exp:(k + 1) * n_exp] + run)
        run = run + totals[k * n_exp:(k + 1) * n_exp]
    bases.append(run)
    return jnp.concatenate(pieces, axis=1), bases


def _route_kernel(aff_ref, upper_ref, slot_ref, base_ref, *, capacity):
    aff = aff_ref[0]
    cap = jnp.float32(capacity)

    def refine(i, thr_bits):
        cand = thr_bits | jnp.left_shift(jnp.int32(1), 30 - i)
        cnt = jnp.sum(jnp.where(aff >= pltpu.bitcast(cand, F32), 1.0, 0.0), axis=1, keepdims=True)
        return jnp.where(cnt >= cap, cand, thr_bits)

    thr_bits = lax.fori_loop(0, 31, refine, jnp.zeros((aff.shape[0], 1), jnp.int32))
    thr = pltpu.bitcast(thr_bits, F32)
    above = jnp.where(aff > thr, 1.0, 0.0)
    tied = jnp.where(aff == thr, 1.0, 0.0)
    need = cap - jnp.sum(above, axis=1, keepdims=True)
    upper = upper_ref[...]
    tie_rank, _ = _prefix_counts(tied, upper)
    chosen = above + tied * jnp.where(tie_rank < need, 1.0, 0.0)
    slot, bases = _prefix_counts(chosen, upper)
    slot_ref[0] = jnp.where(chosen > 0.0, slot, -1.0).astype(jnp.int32)
    lane = lax.broadcasted_iota(jnp.int32, (aff.shape[0], LANES), 1)
    table = jnp.zeros((aff.shape[0], LANES), F32)
    for k, bk in enumerate(bases):
        table = jnp.where(lane == k, bk, table)
    base_ref[0] = table.astype(jnp.int32)


def _route(aff, capacity):
    bsz, n_exp, s = aff.shape
    idx = jnp.arange(MXU_DIM)
    upper = (idx[:, None] < idx[None, :]).astype(BF16)
    per_batch = lambda b: (b, 0, 0)
    return pl.pallas_call(
        functools.partial(_route_kernel, capacity=capacity),
        grid=(bsz,),
        in_specs=[pl.BlockSpec((1, n_exp, s), per_batch), pl.BlockSpec((MXU_DIM, MXU_DIM), lambda b: (0, 0))],
        out_specs=[pl.BlockSpec((1, n_exp, s), per_batch), pl.BlockSpec((1, n_exp, LANES), per_batch)],
        out_shape=[jax.ShapeDtypeStruct((bsz, n_exp, s), jnp.int32),
                   jax.ShapeDtypeStruct((bsz, n_exp, LANES), jnp.int32)],
        compiler_params=_params("arbitrary"),
        name="route",
    )(aff, upper)


def _window_plan(base_ref, row, chunk, capacity):
    first = base_ref[row + chunk]
    end = base_ref[row + chunk + 1]
    start = jnp.minimum((first // ROUTE_ALIGN) * ROUTE_ALIGN, capacity - ROUTE_WINDOW)
    n_windows = (end - start + ROUTE_WINDOW - 1) // ROUTE_WINDOW
    return pl.multiple_of(start, ROUTE_ALIGN), n_windows


def _later_window(start, k, capacity):
    lo = start + k * ROUTE_WINDOW
    return lo, pl.multiple_of(jnp.minimum(lo, capacity - ROUTE_WINDOW), ROUTE_ALIGN)


def _dispatch_kernel(base_ref, h_ref, slot_ref, aff_ref, xe_ref, gs_ref, *, n_exp, n_chunks, capacity):
    b, eg, c = pl.program_id(0), pl.program_id(1), pl.program_id(2)
    group = xe_ref.shape[0]
    chunk = h_ref.shape[1]

    @pl.when(c == 0)
    def _():
        xe_ref[...] = jnp.zeros_like(xe_ref)
        gs_ref[...] = jnp.zeros_like(gs_ref)

    h = h_ref[0]
    w_iota = lax.broadcasted_iota(jnp.int32, (ROUTE_WINDOW, chunk), 0)
    plans, hots = [], []
    for e in range(group):
        row = (b * n_exp + eg * group + e) * (n_chunks + 1)
        start, n_windows = _window_plan(base_ref, row, c, capacity)
        plans.append((start, n_windows))
        hots.append((slot_ref[0, e:e + 1, :] - start) == w_iota)
    stack = jnp.concatenate([jnp.where(hot, 1.0, 0.0) for hot in hots], axis=0).astype(BF16)
    rows = jnp.dot(stack, h, preferred_element_type=F32)
    most_windows = plans[0][1]
    for e in range(group):
        start, n_windows = plans[e]
        most_windows = jnp.maximum(most_windows, n_windows)
        win = pl.ds(start, ROUTE_WINDOW)
        xe_ref[e, 0, win, :] += rows[e * ROUTE_WINDOW:(e + 1) * ROUTE_WINDOW].astype(xe_ref.dtype)
        gs_ref[e, 0, win, :] += jnp.sum(jnp.where(hots[e], aff_ref[0, e:e + 1, :], 0.0), axis=1, keepdims=True)

    @pl.when(most_windows > 1)
    def _():
        for e in range(group):
            start, n_windows = plans[e]

            def more(k, carry, e=e, start=start):
                lo, st = _later_window(start, k, capacity)
                slots = slot_ref[0, e:e + 1, :]
                hot = jnp.logical_and(slots - st == w_iota, slots >= lo)
                extra = jnp.dot(jnp.where(hot, 1.0, 0.0).astype(BF16), h, preferred_element_type=F32)
                xe_ref[e, 0, pl.ds(st, ROUTE_WINDOW), :] += extra.astype(xe_ref.dtype)
                gs_ref[e, 0, pl.ds(st, ROUTE_WINDOW), :] += jnp.sum(
                    jnp.where(hot, aff_ref[0, e:e + 1, :], 0.0), axis=1, keepdims=True)
                return carry

            lax.fori_loop(1, n_windows, more, 0)


def _dispatch(h2, slot, aff, base_flat, capacity):
    bsz, s, d = h2.shape
    n_exp = slot.shape[1]
    chunk = min(ROUTE_CHUNK, s)
    group = min(EXPERT_GROUP, n_exp)
    n_chunks = s // chunk
    return pl.pallas_call(
        functools.partial(_dispatch_kernel, n_exp=n_exp, n_chunks=n_chunks, capacity=capacity),
        grid_spec=pltpu.PrefetchScalarGridSpec(
            num_scalar_prefetch=1,
            grid=(bsz, n_exp // group, n_chunks),
            in_specs=[pl.BlockSpec((1, chunk, d), lambda b, g, c, base: (b, c, 0)),
                      pl.BlockSpec((1, group, chunk), lambda b, g, c, base: (b, g, c)),
                      pl.BlockSpec((1, group, chunk), lambda b, g, c, base: (b, g, c))],
            out_specs=[pl.BlockSpec((group, 1, capacity, d), lambda b, g, c, base: (g, b, 0, 0)),
                       pl.BlockSpec((group, 1, capacity, 1), lambda b, g, c, base: (g, b, 0, 0))]),
        out_shape=[jax.ShapeDtypeStruct((n_exp, bsz, capacity, d), BF16),
                   jax.ShapeDtypeStruct((n_exp, bsz, capacity, 1), F32)],
        compiler_params=_params("arbitrary", "arbitrary", "arbitrary"),
        name="dispatch",
    )(base_flat, h2, slot, aff)


def _expert_kernel(x_ref, g_ref, wg_ref, wu_ref, wd_ref, y_ref):
    x = x_ref[0, 0]
    gate = jnp.dot(x, wg_ref[0], preferred_element_type=F32)
    up = jnp.dot(x, wu_ref[0], preferred_element_type=F32)
    he = (gate * _sigmoid(gate) * up).astype(BF16)
    y_ref[0, 0] = (jnp.dot(he, wd_ref[0], preferred_element_type=F32) * g_ref[0, 0]).astype(y_ref.dtype)


def _experts(xe, gates, w_gate, w_up, w_down):
    n_exp, bsz, cap, d = xe.shape
    ff = w_gate.shape[2]
    row = lambda e, i: (e, i, 0, 0)
    per_expert = lambda e, i: (e, 0, 0)
    return pl.pallas_call(
        _expert_kernel,
        grid=(n_exp, bsz),
        in_specs=[pl.BlockSpec((1, 1, cap, d), row), pl.BlockSpec((1, 1, cap, 1), row),
                  pl.BlockSpec((1, d, ff), per_expert), pl.BlockSpec((1, d, ff), per_expert),
                  pl.BlockSpec((1, ff, d), per_expert)],
        out_specs=pl.BlockSpec((1, 1, cap, d), row),
        out_shape=jax.ShapeDtypeStruct(xe.shape, BF16),
        compiler_params=_params("arbitrary", "arbitrary"),
        name="experts",
    )(xe, gates, w_gate.astype(BF16), w_up.astype(BF16), w_down.astype(BF16))


def _combine_kernel(base_ref, y_ref, slot_ref, x1_ref, gate_ref, out_ref, acc_s, *, n_chunks, capacity):
    b, c = pl.program_id(0), pl.program_id(1)
    n_exp = y_ref.shape[0]
    chunk = x1_ref.shape[1]
    w_iota = lax.broadcasted_iota(jnp.int32, (ROUTE_WINDOW, chunk), 0)
    plans, hots, wins = [], [], []
    for e in range(n_exp):
        start, n_windows = _window_plan(base_ref, (b * n_exp + e) * (n_chunks + 1), c, capacity)
        plans.append((start, n_windows))
        hots.append(jnp.where((slot_ref[0, e:e + 1, :] - start) == w_iota, 1.0, 0.0))
        wins.append(y_ref[e, 0, pl.ds(start, ROUTE_WINDOW), :])
    hot = jnp.concatenate(hots, axis=0).astype(BF16)
    ywin = jnp.concatenate(wins, axis=0)
    acc = lax.dot_general(hot, ywin, (((0,), (0,)), ((), ())), preferred_element_type=F32)
    most_windows = plans[0][1]
    for _, n_windows in plans[1:]:
        most_windows = jnp.maximum(most_windows, n_windows)

    @pl.when(most_windows <= 1)
    def _():
        out_ref[0] = x1_ref[0] + gate_ref[0] * acc

    @pl.when(most_windows > 1)
    def _():
        acc_s[...] = acc
        for e in range(n_exp):
            start, n_windows = plans[e]

            def more(k, carry, e=e, start=start):
                lo, st = _later_window(start, k, capacity)
                slots = slot_ref[0, e:e + 1, :]
                sel = jnp.logical_and(slots - st == w_iota, slots >= lo)
                acc_s[...] += lax.dot_general(jnp.where(sel, 1.0, 0.0).astype(BF16),
                                              y_ref[e, 0, pl.ds(st, ROUTE_WINDOW), :],
                                              (((0,), (0,)), ((), ())), preferred_element_type=F32)
                return carry

            lax.fori_loop(1, n_windows, more, 0)
        out_ref[0] = x1_ref[0] + gate_ref[0] * acc_s[...]


def _combine(ys, slot, base_flat, x1, gate2, capacity):
    bsz, s, d = x1.shape
    n_exp = ys.shape[0]
    chunk = min(ROUTE_CHUNK, s)
    n_chunks = s // chunk
    return pl.pallas_call(
        functools.partial(_combine_kernel, n_chunks=n_chunks, capacity=capacity),
        grid_spec=pltpu.PrefetchScalarGridSpec(
            num_scalar_prefetch=1,
            grid=(bsz, n_chunks),
            in_specs=[pl.BlockSpec((n_exp, 1, capacity, d), lambda b, c, base: (0, b, 0, 0)),
                      pl.BlockSpec((1, n_exp, chunk), lambda b, c, base: (b, 0, c)),
                      pl.BlockSpec((1, chunk, d), lambda b, c, base: (b, c, 0)),
                      pl.BlockSpec((1, 1, d), lambda b, c, base: (b, 0, 0))],
            out_specs=pl.BlockSpec((1, chunk, d), lambda b, c, base: (b, c, 0)),
            scratch_shapes=[pltpu.VMEM((chunk, d), F32)]),
        out_shape=jax.ShapeDtypeStruct(x1.shape, F32),
        compiler_params=_params("arbitrary", "arbitrary"),
        name="combine",
    )(base_flat, ys, slot, x1, gate2.reshape(bsz, 1, d))


def _layer(x, c, positions, w_ada, b_ada, norm1_w, w_in, q_norm_w, k_norm_w, conv_w, conv_b, lru_w_a, lru_b_a,
           lru_w_x, lru_b_x, lru_lambda, w_attn_branch, w_lru_branch, w_out, norm2_w, w_router, w_gate, w_up,
           w_down):
    bsz, s, d = x.shape
    mod = _ada(c, w_ada, b_ada)
    shift1, scale1, gate1, shift2, scale2, gate2 = jnp.split(mod, 6, axis=-1)

    n_qkv = 3 * ATTN_WIDTH
    q0, q1, q2, k0, k1, k2, v0, v1, v2 = _inproj(x, positions, shift1, scale1, norm1_w,
                                                 w_in[:, :n_qkv].astype(BF16), q_norm_w, k_norm_w)
    col_scale = jnp.where(jnp.arange(w_in.shape[1] - n_qkv) >= 2 * d, 0.5, 1.0).astype(F32)
    xr, yr, ga, gl = _inproj_wide(x, shift1, scale1, norm1_w, (w_in[:, n_qkv:] * col_scale).astype(BF16))

    outs, lses = [], []
    for (window, dilation), q, k, v in zip(ATTN_GROUPS, (q0, q1, q2), (k0, k1, k2), (v0, v1, v2)):
        o, lse = _attention_group(q, k, v, window, dilation)
        outs.append(o)
        lses.append(lse)

    h_fwd = _lru_pass(xr, conv_w, conv_b, lru_w_a[0], lru_b_a[0], lru_w_x[0], lru_b_x[0], lru_lambda[0],
                      reverse=False)
    lru = _lru_pass(xr, conv_w, conv_b, lru_w_a[1], lru_b_a[1], lru_w_x[1], lru_b_x[1], lru_lambda[1],
                    reverse=True, h_fwd=h_fwd, yr=yr)

    x1, h2, aff = _merge(outs, lses, lru, ga, gl, x, gate1, w_attn_branch, w_lru_branch, w_out, norm2_w,
                         shift2, scale2, w_router)

    capacity = max(1, CAPACITY_FACTOR * s // N_EXPERTS)
    n_chunks = s // min(ROUTE_CHUNK, s)
    slot, base_table = _route(aff, capacity)
    base_flat = base_table[:, :, :n_chunks + 1].reshape(-1)
    xe, gates = _dispatch(h2, slot, aff, base_flat, capacity)
    ys = _experts(xe, gates, w_gate, w_up, w_down)
    return _combine(ys, slot, base_flat, x1, gate2, capacity)


def kernel(x, c, positions, w_ada, b_ada, norm1_w, w_in, q_norm_w, k_norm_w, conv_w, conv_b, lru_w_a, lru_b_a,
           lru_w_x, lru_b_x, lru_lambda, w_attn_branch, w_lru_branch, w_out, norm2_w, w_router, w_gate, w_up,
           w_down):
    for l in range(w_ada.shape[0]):
        x = _layer(x, c, positions, w_ada[l], b_ada[l], norm1_w[l], w_in[l], q_norm_w[l], k_norm_w[l],
                   conv_w[l, :, 0, :], conv_b[l], lru_w_a[l], lru_b_a[l], lru_w_x[l], lru_b_x[l], lru_lambda[l],
                   w_attn_branch[l], w_lru_branch[l], w_out[l], norm2_w[l], w_router[l], w_gate[l], w_up[l],
                   w_down[l])
    return x
```

```python
import functools

import jax
import jax.numpy as jnp
from jax import lax
from jax.experimental import pallas as pl
from jax.experimental.pallas import tpu as pltpu

F32 = jnp.float32
BF16 = jnp.bfloat16

HEAD_DIM = 64
HEADS_PER_GROUP = 4
GROUP_WIDTH = HEADS_PER_GROUP * HEAD_DIM
ATTN_GROUPS = ((128, 1), (512, 4), (2048, 16))
ATTN_WIDTH = GROUP_WIDTH * len(ATTN_GROUPS)
LRU_C = 8.0
CONV_WIDTH = 4
N_EXPERTS = 16
CAPACITY_FACTOR = 2
ROPE_THETA = 10000.0
NORM_EPS = 1e-6
NEG_INF = -1e30
GELU_C0 = 0.7978845608028654
GELU_C1 = 0.044715
LOG2_E = 1.4426950408889634
TINY = 1e-30

LANES = 128
MXU_DIM = 256
VMEM_LIMIT = 56 * 1024 * 1024

ROW_TILE = 512
WIDE_ROW_TILE = 1024
ATTN_TQ = 128
ATTN_CLASSES_PER_STEP = 4
LRU_TC = 128
LRU_NB = 8
HALO = 16
ROUTE_CHUNK = 256
ROUTE_WINDOW = 64
ROUTE_ALIGN = 16
EXPERT_GROUP = 16
ROPE_PACK = LANES // (HEAD_DIM // 2)


def _sigmoid(t):
    return 0.5 * jnp.tanh(0.5 * t) + 0.5


def _params(*sem):
    return pltpu.CompilerParams(dimension_semantics=sem, vmem_limit_bytes=VMEM_LIMIT)


def _ada_kernel(c_ref, w_ref, b_ref, o_ref):
    c = c_ref[...]
    o_ref[...] = jnp.dot(c * _sigmoid(c), w_ref[...], preferred_element_type=F32,
                         precision=lax.Precision.HIGHEST) + b_ref[...]


def _ada(c, w, b):
    bsz, d = c.shape
    n = w.shape[1]
    tn = n // 4
    return pl.pallas_call(
        _ada_kernel,
        grid=(n // tn,),
        in_specs=[pl.BlockSpec((bsz, d), lambda j: (0, 0)),
                  pl.BlockSpec((d, tn), lambda j: (0, j)),
                  pl.BlockSpec((1, tn), lambda j: (0, j))],
        out_specs=pl.BlockSpec((bsz, tn), lambda j: (0, j)),
        out_shape=jax.ShapeDtypeStruct((bsz, n), F32),
        compiler_params=_params("arbitrary"),
        name="ada",
    )(c, w, b.reshape(1, n))


def _store_classes(val, out_ref, relayout_s, dilation):
    if dilation == 1:
        out_ref[0] = val.astype(out_ref.dtype)
        return
    n_tiles = GROUP_WIDTH // LANES
    rows = val.shape[0] // dilation
    for t in range(n_tiles):
        relayout_s[t] = val[:, t * LANES:(t + 1) * LANES]
    for r in range(dilation):
        for t in range(n_tiles):
            lo = r * GROUP_WIDTH + t * LANES
            out_ref[0, :, lo:lo + LANES] = relayout_s[t, pl.ds(r, rows, stride=dilation), :].astype(out_ref.dtype)


def _inproj_kernel(x_ref, pos_ref, shift_ref, gain_ref, w_ref, qnw_ref, knw_ref, invf_ref, seg_ref,
                   spread_ref, q0_ref, q1_ref, q2_ref, k0_ref, k1_ref, k2_ref, v0_ref, v1_ref, v2_ref,
                   relayout_s, cos_s, sin_s):
    tm = x_ref.shape[1]
    lane = lax.broadcasted_iota(jnp.int32, (tm, GROUP_WIDTH), 1)
    first_half = (lane & (HEAD_DIM // 2)) == 0
    seg = seg_ref[...]

    x = x_ref[0]
    ms = jnp.mean(x * x, axis=-1, keepdims=True)
    hb = (x * lax.rsqrt(ms + NORM_EPS) * gain_ref[0] + shift_ref[0]).astype(BF16)

    packed_rows = tm // ROPE_PACK
    ang = pos_ref[0].astype(F32) * invf_ref[...]
    cos_p = jnp.cos(ang)
    sin_p = jnp.sin(ang)
    for p in range(ROPE_PACK):
        rows_p = pl.ds(p, packed_rows, stride=ROPE_PACK)
        cos_s[rows_p, :] = jnp.dot(cos_p, spread_ref[p], preferred_element_type=F32,
                                   precision=lax.Precision.HIGHEST)
        sin_s[rows_p, :] = jnp.dot(sin_p, spread_ref[p], preferred_element_type=F32,
                                   precision=lax.Precision.HIGHEST)
    cos = cos_s[...]
    sin = sin_s[...]
    cos2 = jnp.concatenate([cos, cos], axis=1)
    sin2 = jnp.concatenate([sin, sin], axis=1)
    sin2 = jnp.where(first_half, -sin2, sin2)

    def project(off, j):
        lo_c = off + j * GROUP_WIDTH
        t = jnp.dot(hb, w_ref[:, lo_c:lo_c + GROUP_WIDTH], preferred_element_type=F32)
        return t, (t * t).astype(BF16)

    def norm_rope(projected, w_norm_ref, out_ref, j):
        t, squares = projected
        lo_c = j * GROUP_WIDTH
        msq = jnp.dot(squares, seg, preferred_element_type=F32) * (1.0 / HEAD_DIM)
        y = t * lax.rsqrt(msq + NORM_EPS) * w_norm_ref[:, lo_c:lo_c + GROUP_WIDTH]
        rot = jnp.where(first_half, pltpu.roll(y, GROUP_WIDTH - HEAD_DIM // 2, 1),
                        pltpu.roll(y, HEAD_DIM // 2, 1))
        _store_classes(y * cos2 + rot * sin2, out_ref, relayout_s, ATTN_GROUPS[j][1])

    def value(out_ref, j):
        off = 2 * ATTN_WIDTH + j * GROUP_WIDTH
        v = jnp.dot(hb, w_ref[:, off:off + GROUP_WIDTH], preferred_element_type=F32)
        _store_classes(v, out_ref, relayout_s, ATTN_GROUPS[j][1])

    tiles = [(0, qnw_ref, q0_ref, 0), (ATTN_WIDTH, knw_ref, k0_ref, 0), (0, qnw_ref, q1_ref, 1),
             (ATTN_WIDTH, knw_ref, k1_ref, 1), (0, qnw_ref, q2_ref, 2), (ATTN_WIDTH, knw_ref, k2_ref, 2)]
    values = [(v0_ref, 0), (v1_ref, 1), (v2_ref, 2)]
    ahead = project(tiles[0][0], tiles[0][3])
    for n, (off, w_norm_ref, out_ref, j) in enumerate(tiles):
        current = ahead
        if n + 1 < len(tiles):
            ahead = project(tiles[n + 1][0], tiles[n + 1][3])
        if n < len(values):
            value(*values[n])
        norm_rope(current, w_norm_ref, out_ref, j)


def _inproj_wide_kernel(x_ref, shift_ref, gain_ref, w_ref, *out_refs):
    x = x_ref[0]
    d = x.shape[1]
    ms = jnp.mean(x * x, axis=-1, keepdims=True)
    hb = (x * lax.rsqrt(ms + NORM_EPS) * gain_ref[0] + shift_ref[0]).astype(BF16)
    for i, out_ref in enumerate(out_refs):
        out_ref[0] = jnp.dot(hb, w_ref[:, i * d:(i + 1) * d], preferred_element_type=F32).astype(out_ref.dtype)


def _inproj_wide(x, shift, gain, w_wide):
    bsz, s, d = x.shape
    n_out = w_wide.shape[1] // d
    tm = min(WIDE_ROW_TILE, s)
    row = lambda b, i: (b, i, 0)
    per_batch = lambda b, i: (b, 0, 0)
    const = lambda b, i: (0, 0)
    return pl.pallas_call(
        _inproj_wide_kernel,
        grid=(bsz, s // tm),
        in_specs=[pl.BlockSpec((1, tm, d), row),
                  pl.BlockSpec((1, 1, d), per_batch),
                  pl.BlockSpec((1, 1, d), per_batch),
                  pl.BlockSpec(w_wide.shape, const, pipeline_mode=pl.Buffered(1))],
        out_specs=[pl.BlockSpec((1, tm, d), row)] * n_out,
        out_shape=[jax.ShapeDtypeStruct((bsz, s, d), BF16)] * n_out,
        compiler_params=_params("arbitrary", "arbitrary"),
        name="inproj_wide",
    )(x, shift.reshape(bsz, 1, d), gain.reshape(bsz, 1, d), w_wide)


def _inproj(x, positions, shift, gain, w_in, q_norm_w, k_norm_w):
    bsz, s, d = x.shape
    tm = min(ROW_TILE, s)
    half = HEAD_DIM // 2
    inv_freq = ROPE_THETA ** (-jnp.arange(half, dtype=F32) / half)
    inv_freq = jnp.tile(inv_freq, ROPE_PACK).reshape(1, LANES)
    pos_packed = jnp.repeat(positions.reshape(bsz, s // ROPE_PACK, ROPE_PACK), half, axis=-1)
    lane = jnp.arange(LANES)
    spread = (lane[None, :, None] == (jnp.arange(ROPE_PACK)[:, None, None] * half + lane[None, None, :] % half))
    spread = spread.astype(F32)
    head_of = jnp.arange(GROUP_WIDTH) // HEAD_DIM
    seg = (head_of[:, None] == head_of[None, :]).astype(BF16)
    row = lambda b, i: (b, i, 0)
    per_batch = lambda b, i: (b, 0, 0)
    const = lambda b, i: (0, 0)
    narrow = [jax.ShapeDtypeStruct((bsz, s // dil, dil * GROUP_WIDTH), BF16) for _, dil in ATTN_GROUPS] * 3
    narrow_specs = [pl.BlockSpec((1, tm // dil, dil * GROUP_WIDTH), row) for _, dil in ATTN_GROUPS] * 3
    return pl.pallas_call(
        _inproj_kernel,
        grid=(bsz, s // tm),
        in_specs=[pl.BlockSpec((1, tm, d), row),
                  pl.BlockSpec((1, tm // ROPE_PACK, LANES), row),
                  pl.BlockSpec((1, 1, d), per_batch),
                  pl.BlockSpec((1, 1, d), per_batch),
                  pl.BlockSpec(w_in.shape, const),
                  pl.BlockSpec((1, ATTN_WIDTH), const),
                  pl.BlockSpec((1, ATTN_WIDTH), const),
                  pl.BlockSpec((1, LANES), const),
                  pl.BlockSpec((GROUP_WIDTH, GROUP_WIDTH), const),
                  pl.BlockSpec(spread.shape, lambda b, i: (0, 0, 0))],
        out_specs=narrow_specs,
        out_shape=narrow,
        scratch_shapes=[pltpu.VMEM((GROUP_WIDTH // LANES, tm, LANES), F32),
                        pltpu.VMEM((tm, LANES), F32), pltpu.VMEM((tm, LANES), F32)],
        compiler_params=_params("arbitrary", "arbitrary"),
        name="inproj",
    )(x, pos_packed, shift.reshape(bsz, 1, d), gain.reshape(bsz, 1, d), w_in,
      (q_norm_w * HEAD_DIM ** -0.5).reshape(1, ATTN_WIDTH),
      k_norm_w.reshape(1, ATTN_WIDTH), inv_freq, seg, spread)


def _attn_kernel(q_ref, k_ref, v_ref, o_ref, lse_ref, *, length, tq, tk, half):
    rel0 = (lax.broadcasted_iota(jnp.int32, (2 * tq, tk), 1)
            - (lax.broadcasted_iota(jnp.int32, (2 * tq, tk), 0) & (tq - 1)))
    lane = lax.broadcasted_iota(jnp.int32, (tq, LANES), 1)
    head_a = lane < HEAD_DIM
    n_tiles = length // tq

    per_trip = 4 if n_tiles % 4 == 0 else (2 if n_tiles % 2 == 0 else 1)

    def scores(item):
        t0, ws, _, cols = item
        qp = q_ref[0, pl.ds(t0, tq), cols]
        kp = k_ref[0, pl.ds(ws, tk), cols]
        zero = jnp.zeros_like(qp)
        q2 = jnp.concatenate([jnp.where(head_a, qp, zero), jnp.where(head_a, zero, qp)], axis=0)
        return lax.dot_general(q2, kp, (((1,), (1,)), ((), ())), preferred_element_type=F32)

    def trip(g, carry, col0):
        items = []
        for u in range(per_trip):
            t0 = pl.multiple_of((g * per_trip + u) * tq, tq)
            ws = pl.multiple_of(jnp.clip(t0 - half, 0, length - tk), half)
            valid = jnp.abs(rel0 + (ws - t0)) <= half
            for p in range(GROUP_WIDTH // LANES):
                items.append((t0, ws, valid, slice(col0 + p * LANES, col0 + (p + 1) * LANES)))
        ahead = scores(items[0])
        for n, (t0, ws, valid, cols) in enumerate(items):
            s = ahead
            if n + 1 < len(items):
                ahead = scores(items[n + 1])
            s = jnp.where(valid, s, NEG_INF)
            m = jnp.max(s, axis=1, keepdims=True)
            e = jnp.exp(s - m)
            den = jnp.sum(e, axis=1, keepdims=True)
            vp = v_ref[0, pl.ds(ws, tk), cols]
            o = jnp.dot(e.astype(BF16), vp, preferred_element_type=F32) * (1.0 / den)
            lse = jnp.broadcast_to(m + jnp.log(den), (2 * tq, LANES))
            o_ref[0, pl.ds(t0, tq), cols] = jnp.where(head_a, o[:tq], o[tq:]).astype(o_ref.dtype)
            lse_ref[0, pl.ds(t0, tq), cols] = jnp.where(head_a, lse[:tq], lse[tq:])
        return carry

    for cls in range(q_ref.shape[2] // GROUP_WIDTH):
        lax.fori_loop(0, n_tiles // per_trip, functools.partial(trip, col0=cls * GROUP_WIDTH), 0)


def _attention_group(q, k, v, window, dilation):
    bsz, length, _ = q.shape
    half = window // (2 * dilation)
    tq = min(ATTN_TQ, length)
    assert tq & (tq - 1) == 0, "query tile must be a power of two"
    tk = min(tq + 2 * half, length)
    classes = min(dilation, ATTN_CLASSES_PER_STEP)
    spec = pl.BlockSpec((1, length, classes * GROUP_WIDTH), lambda b, r: (b, 0, r))
    return pl.pallas_call(
        functools.partial(_attn_kernel, length=length, tq=tq, tk=tk, half=half),
        grid=(bsz, dilation // classes),
        in_specs=[spec] * 3,
        out_specs=[spec] * 2,
        out_shape=[jax.ShapeDtypeStruct(q.shape, BF16), jax.ShapeDtypeStruct(q.shape, F32)],
        compiler_params=_params("arbitrary", "arbitrary"),
        name=f"attn_d{dilation}",
    )(q, k, v)


def _gelu_tanh(t):
    return 0.5 * t * (1.0 + jnp.tanh(GELU_C0 * (t + GELU_C1 * t * t * t)))


def _lru_kernel(*refs, reverse, final, n_chunks):
    if final:
        (xr_ref, xp_ref, xn_ref, hf_ref, yr_ref, shift_ref, cw_ref, cb_ref, wa_ref, wx_ref, ba_ref, bx_ref, lam_ref,
         out_ref, a_s, u_s, h_s, state_s) = refs
    else:
        (xr_ref, xp_ref, xn_ref, shift_ref, cw_ref, cb_ref, wa_ref, wx_ref, ba_ref, bx_ref, lam_ref,
         out_ref, a_s, u_s, h_s, state_s) = refs
    nb, tc, ch = xr_ref.shape
    n_slabs = ch // LANES
    step = pl.program_id(1)
    chunk = (n_chunks - 1 - step) if reverse else step

    @pl.when(step == 0)
    def _():
        state_s[...] = jnp.zeros_like(state_s)

    lam = lam_ref[...]
    neg_c_softplus = -LRU_C * (jnp.maximum(-lam, 0.0) + jnp.log(1.0 + jnp.exp(-jnp.abs(lam))))
    half_log2_decay = (0.5 * LOG2_E) * neg_c_softplus
    pad_rows = jnp.zeros((shift_ref.shape[1] - tc - 2 * HALO, ch), BF16)

    rows_per_trip = 2 if nb % 2 == 0 else 1
    n_tiles = ch // MXU_DIM
    tile_cols = [slice(kk * MXU_DIM, (kk + 1) * MXU_DIM) for kk in range(n_tiles)]
    half_cw = 0.5 * cw_ref[...]
    half_cb = 0.5 * cb_ref[...]

    def coeffs(trip, carry):
        batch_rows = [trip * rows_per_trip + r for r in range(rows_per_trip)]
        xcs = []
        for b in batch_rows:
            before = jnp.where(chunk > 0, xp_ref[b], jnp.zeros((HALO, ch), BF16))
            after = jnp.where(chunk < n_chunks - 1, xn_ref[b], jnp.zeros((HALO, ch), BF16))
            stack = jnp.concatenate([before, xr_ref[b], after, pad_rows], axis=0)
            for cols in tile_cols:
                taps = jnp.dot(shift_ref[...], stack[:, cols], preferred_element_type=F32)
                half_x = half_cb[:, cols]
                for j in range(CONV_WIDTH):
                    half_x = half_x + half_cw[j:j + 1, cols] * taps[j * tc:(j + 1) * tc]
                xcs.append(half_x)
        pre = []
        for n, half_x in enumerate(xcs):
            kk = n % n_tiles
            xcb = half_x.astype(BF16)
            pre.append((jnp.dot(xcb, wa_ref[kk], preferred_element_type=F32),
                        jnp.dot(xcb, wx_ref[kk], preferred_element_type=F32)))
        for n, half_x in enumerate(xcs):
            kk, cols = n % n_tiles, tile_cols[n % n_tiles]
            slab_rows = pl.ds(batch_rows[n // n_tiles], tc, stride=nb)
            tanh_r = jnp.tanh(pre[n][0] + ba_ref[:, cols])
            tanh_i = jnp.tanh(pre[n][1] + bx_ref[:, cols])
            a = jnp.exp2(tanh_r * half_log2_decay[:, cols] + half_log2_decay[:, cols])
            v = 1.0 - a * a
            root = v * lax.rsqrt(jnp.maximum(v, TINY))
            u = root * (tanh_i * half_x + half_x)
            for t in range(MXU_DIM // LANES):
                j = kk * (MXU_DIM // LANES) + t
                a_s[j, slab_rows, :] = a[:, t * LANES:(t + 1) * LANES]
                u_s[j, slab_rows, :] = u[:, t * LANES:(t + 1) * LANES]
        return carry

    lax.fori_loop(0, nb // rows_per_trip, coeffs, 0)

    def scan_step(i, hs):
        t = (tc - 1 - i) if reverse else i
        new = []
        for j in range(n_slabs):
            rows = pl.ds(pl.multiple_of(t * nb, nb), nb)
            hj = a_s[j, rows, :] * hs[j] + u_s[j, rows, :]
            h_s[j, rows, :] = hj
            new.append(hj)
        return tuple(new)

    hs = lax.fori_loop(0, tc, scan_step, tuple(state_s[j] for j in range(n_slabs)), unroll=2)
    for j in range(n_slabs):
        state_s[j] = hs[j]

    def emit(b, carry):
        hb = jnp.concatenate([h_s[j, pl.ds(b, tc, stride=nb), :] for j in range(n_slabs)], axis=1)
        if final:
            hb = (hb + hf_ref[b].astype(F32)) * _gelu_tanh(yr_ref[b].astype(F32))
        out_ref[b] = hb.astype(out_ref.dtype)
        return carry

    lax.fori_loop(0, nb, emit, 0, unroll=2 if nb % 2 == 0 else 1)


def _block_diag_tiles(w):
    per_tile = MXU_DIM // w.shape[1]
    n_tiles = w.shape[0] // per_tile
    w = w.reshape(n_tiles, per_tile, w.shape[1], w.shape[2])
    eye = jnp.eye(per_tile, dtype=w.dtype)
    tiles = jnp.einsum('tpkj,pq->tpkqj', w, eye)
    return tiles.reshape(n_tiles, MXU_DIM, MXU_DIM).astype(BF16)


def _lru_pass(xr, conv_w, conv_b, w_a, b_a, w_x, b_x, lam, *, reverse, h_fwd=None, yr=None):
    bsz, s, ch = xr.shape
    nb = min(LRU_NB, bsz)
    tc = min(LRU_TC, s)
    n_chunks = s // tc
    final = h_fwd is not None
    chunk_of = (lambda c: n_chunks - 1 - c) if reverse else (lambda c: c)
    per_halo = tc // HALO
    last_halo = s // HALO - 1
    main = pl.BlockSpec((nb, tc, ch), lambda g, c: (g, chunk_of(c), 0))
    prev = pl.BlockSpec((nb, HALO, ch), lambda g, c: (g, jnp.maximum(chunk_of(c) * per_halo - 1, 0), 0))
    nxt = pl.BlockSpec((nb, HALO, ch), lambda g, c: (g, jnp.minimum((chunk_of(c) + 1) * per_halo, last_halo), 0))
    const2 = lambda g, c: (0, 0)
    const3 = lambda g, c: (0, 0, 0)
    vec = pl.BlockSpec((1, ch), const2)
    tiles = pl.BlockSpec((ch // MXU_DIM, MXU_DIM, MXU_DIM), const3)
    acts = [xr, xr, xr] + ([h_fwd, yr] if final else [])
    act_specs = [main, prev, nxt] + ([main, main] if final else [])
    slab = pltpu.VMEM((ch // LANES, tc * nb, LANES), F32)
    stack_rows = -(-(tc + 2 * HALO) // MXU_DIM) * MXU_DIM
    tap_row = HALO - CONV_WIDTH // 2 + jnp.arange(CONV_WIDTH)[:, None] + jnp.arange(tc)[None, :]
    shifts = (tap_row.reshape(-1, 1) == jnp.arange(stack_rows)[None, :]).astype(BF16)
    return pl.pallas_call(
        functools.partial(_lru_kernel, reverse=reverse, final=final, n_chunks=n_chunks),
        grid=(bsz // nb, n_chunks),
        in_specs=act_specs + [pl.BlockSpec(shifts.shape, const2), pl.BlockSpec((CONV_WIDTH, ch), const2), vec,
                              tiles, tiles, vec, vec, vec],
        out_specs=main,
        out_shape=jax.ShapeDtypeStruct((bsz, s, ch), BF16),
        scratch_shapes=[slab, slab, slab, pltpu.VMEM((ch // LANES, nb, LANES), F32)],
        compiler_params=_params("arbitrary", "arbitrary"),
        name="lru_bwd" if reverse else "lru_fwd",
    )(*acts, shifts, conv_w.reshape(CONV_WIDTH, ch), conv_b.reshape(1, ch), _block_diag_tiles(w_a),
      _block_diag_tiles(w_x), (0.5 * b_a).reshape(1, ch), (0.5 * b_x).reshape(1, ch), lam.reshape(1, ch))


def _load_classes(in_ref, relayout_s, dilation):
    if dilation == 1:
        return in_ref[0].astype(F32)
    n_tiles = GROUP_WIDTH // LANES
    rows = in_ref.shape[1]
    for r in range(dilation):
        for t in range(n_tiles):
            lo = r * GROUP_WIDTH + t * LANES
            relayout_s[t, pl.ds(r, rows, stride=dilation), :] = in_ref[0, :, lo:lo + LANES].astype(F32)
    return jnp.concatenate([relayout_s[t] for t in range(n_tiles)], axis=1)


def _merge_kernel(o0_ref, o1_ref, o2_ref, l0_ref, l1_ref, l2_ref, lru_ref, ga_ref, gl_ref, x_ref, gate_ref,
                  wab_ref, wlb_ref, wout_ref, shift_ref, gain_ref, wr_ref,
                  x1_ref, h2_ref, aff_ref, so1_s, so2_s, sl1_s, sl2_s):
    dil = [dilation for _, dilation in ATTN_GROUPS]
    l0 = _load_classes(l0_ref, None, dil[0])
    l1 = _load_classes(l1_ref, sl1_s, dil[1])
    l2 = _load_classes(l2_ref, sl2_s, dil[2])
    o0 = _load_classes(o0_ref, None, dil[0])
    o1 = _load_classes(o1_ref, so1_s, dil[1])
    o2 = _load_classes(o2_ref, so2_s, dil[2])
    m = jnp.maximum(jnp.maximum(l0, l1), l2)
    e0, e1, e2 = jnp.exp(l0 - m), jnp.exp(l1 - m), jnp.exp(l2 - m)
    attn = (e0 * o0 + e1 * o1 + e2 * o2) / (e0 + e1 + e2)
    branch_a = jnp.dot(attn.astype(BF16), wab_ref[...], preferred_element_type=F32)
    branch_l = jnp.dot(lru_ref[0], wlb_ref[...], preferred_element_type=F32)
    merged = ((jnp.tanh(ga_ref[0].astype(F32)) + 1.0) * branch_a
              + (jnp.tanh(gl_ref[0].astype(F32)) + 1.0) * branch_l)
    mix = jnp.dot(merged.astype(BF16), wout_ref[...], preferred_element_type=F32)
    x1 = x_ref[0] + gate_ref[0] * mix
    x1_ref[0] = x1
    ms = jnp.mean(x1 * x1, axis=-1, keepdims=True)
    h2 = x1 * lax.rsqrt(ms + NORM_EPS) * gain_ref[0] + shift_ref[0]
    h2_ref[0] = h2.astype(h2_ref.dtype)
    logits = lax.dot_general(wr_ref[...], h2, (((1,), (1,)), ((), ())), preferred_element_type=F32,
                             precision=lax.Precision.HIGHEST)
    z = jnp.exp(logits - jnp.max(logits, axis=0, keepdims=True))
    aff_ref[0] = z / jnp.sum(z, axis=0, keepdims=True)


def _merge(outs, lses, lru, ga, gl, x, gate1, w_ab, w_lb, w_out, shift2, gain2, w_router):
    bsz, s, d = x.shape
    tm = min(ROW_TILE, s)
    n_exp = w_router.shape[1]
    row = lambda b, i: (b, i, 0)
    per_batch = lambda b, i: (b, 0, 0)
    const = lambda b, i: (0, 0)
    grp = [pl.BlockSpec((1, tm // dil, dil * GROUP_WIDTH), row) for _, dil in ATTN_GROUPS]
    wide = pl.BlockSpec((1, tm, d), row)
    mod = pl.BlockSpec((1, 1, d), per_batch)
    full = lambda a: pl.BlockSpec(a.shape, const)
    w_ab, w_lb, w_out = (0.5 * w_ab).astype(BF16), (0.5 * w_lb).astype(BF16), w_out.astype(BF16)
    w_rt = w_router.T
    return pl.pallas_call(
        _merge_kernel,
        grid=(bsz, s // tm),
        in_specs=grp * 2 + [wide] * 4 + [mod, full(w_ab), full(w_lb), full(w_out), mod, mod, full(w_rt)],
        out_specs=[wide, wide, pl.BlockSpec((1, n_exp, tm), lambda b, i: (b, 0, i))],
        out_shape=[jax.ShapeDtypeStruct((bsz, s, d), F32), jax.ShapeDtypeStruct((bsz, s, d), BF16),
                   jax.ShapeDtypeStruct((bsz, n_exp, s), F32)],
        scratch_shapes=[pltpu.VMEM((GROUP_WIDTH // LANES, tm, LANES), F32)] * 4,
        compiler_params=_params("arbitrary", "arbitrary"),
        name="merge",
    )(*outs, *lses, lru, ga, gl, x, gate1.reshape(bsz, 1, d), w_ab, w_lb, w_out, shift2.reshape(bsz, 1, d),
      gain2.reshape(bsz, 1, d), w_rt)


def _prefix_counts(flags, strict_upper):
    n_exp, s = flags.shape
    n_tiles = s // MXU_DIM
    stacked = jnp.concatenate([flags[:, k * MXU_DIM:(k + 1) * MXU_DIM] for k in range(n_tiles)], axis=0)
    within = jnp.dot(stacked.astype(BF16), strict_upper, preferred_element_type=F32)
    totals = jnp.sum(stacked, axis=1, keepdims=True)
    run = jnp.zeros((n_exp, 1), F32)
    pieces, bases = [], []
    for k in range(n_tiles):
        bases.append(run)
        pieces.append(within[k * n_exp:(k + 1) * n_exp] + run)
        run = run + totals[k * n_exp:(k + 1) * n_exp]
    bases.append(run)
    return jnp.concatenate(pieces, axis=1), bases


def _route_kernel(aff_ref, upper_ref, slot_ref, base_ref, *, capacity):
    aff = aff_ref[0]
    cap = jnp.float32(capacity)

    def refine(i, thr_bits):
        cand = thr_bits | jnp.left_shift(jnp.int32(1), 30 - i)
        cnt = jnp.sum(jnp.where(aff >= pltpu.bitcast(cand, F32), 1.0, 0.0), axis=1, keepdims=True)
        return jnp.where(cnt >= cap, cand, thr_bits)

    thr_bits = lax.fori_loop(0, 31, refine, jnp.zeros((aff.shape[0], 1), jnp.int32))
    thr = pltpu.bitcast(thr_bits, F32)
    above = jnp.where(aff > thr, 1.0, 0.0)
    tied = jnp.where(aff == thr, 1.0, 0.0)
    need = cap - jnp.sum(above, axis=1, keepdims=True)
    upper = upper_ref[...]
    tie_rank, _ = _prefix_counts(tied, upper)
    chosen = above + tied * jnp.where(tie_rank < need, 1.0, 0.0)
    slot, bases = _prefix_counts(chosen, upper)
    slot_ref[0] = jnp.where(chosen > 0.0, slot, -1.0).astype(jnp.int32)
    lane = lax.broadcasted_iota(jnp.int32, (aff.shape[0], LANES), 1)
    table = jnp.zeros((aff.shape[0], LANES), F32)
    for k, bk in enumerate(bases):
        table = jnp.where(lane == k, bk, table)
    base_ref[0] = table.astype(jnp.int32)


def _route(aff, capacity):
    bsz, n_exp, s = aff.shape
    idx = jnp.arange(MXU_DIM)
    upper = (idx[:, None] < idx[None, :]).astype(BF16)
    per_batch = lambda b: (b, 0, 0)
    return pl.pallas_call(
        functools.partial(_route_kernel, capacity=capacity),
        grid=(bsz,),
        in_specs=[pl.BlockSpec((1, n_exp, s), per_batch), pl.BlockSpec((MXU_DIM, MXU_DIM), lambda b: (0, 0))],
        out_specs=[pl.BlockSpec((1, n_exp, s), per_batch), pl.BlockSpec((1, n_exp, LANES), per_batch)],
        out_shape=[jax.ShapeDtypeStruct((bsz, n_exp, s), jnp.int32),
                   jax.ShapeDtypeStruct((bsz, n_exp, LANES), jnp.int32)],
        compiler_params=_params("arbitrary"),
        name="route",
    )(aff, upper)


def _window_plan(base_ref, row, chunk, capacity):
    first = base_ref[row + chunk]
    end = base_ref[row + chunk + 1]
    start = jnp.minimum((first // ROUTE_ALIGN) * ROUTE_ALIGN, capacity - ROUTE_WINDOW)
    n_windows = (end - start + ROUTE_WINDOW - 1) // ROUTE_WINDOW
    return pl.multiple_of(start, ROUTE_ALIGN), n_windows


def _later_window(start, k, capacity):
    lo = start + k * ROUTE_WINDOW
    return lo, pl.multiple_of(jnp.minimum(lo, capacity - ROUTE_WINDOW), ROUTE_ALIGN)


def _dispatch_kernel(base_ref, h_ref, slot_ref, aff_ref, xe_ref, gs_ref, *, n_exp, n_chunks, capacity):
    b, eg, c = pl.program_id(0), pl.program_id(1), pl.program_id(2)
    group = xe_ref.shape[0]
    chunk = h_ref.shape[1]

    @pl.when(c == 0)
    def _():
        xe_ref[...] = jnp.zeros_like(xe_ref)
        gs_ref[...] = jnp.zeros_like(gs_ref)

    h = h_ref[0]
    w_iota = lax.broadcasted_iota(jnp.int32, (ROUTE_WINDOW, chunk), 0)
    plans, hots = [], []
    for e in range(group):
        row = (b * n_exp + eg * group + e) * (n_chunks + 1)
        start, n_windows = _window_plan(base_ref, row, c, capacity)
        plans.append((start, n_windows))
        hots.append((slot_ref[0, e:e + 1, :] - start) == w_iota)
    stack = jnp.concatenate([jnp.where(hot, 1.0, 0.0) for hot in hots], axis=0).astype(BF16)
    rows = jnp.dot(stack, h, preferred_element_type=F32)
    most_windows = plans[0][1]
    for e in range(group):
        start, n_windows = plans[e]
        most_windows = jnp.maximum(most_windows, n_windows)
        win = pl.ds(start, ROUTE_WINDOW)
        xe_ref[e, 0, win, :] += rows[e * ROUTE_WINDOW:(e + 1) * ROUTE_WINDOW].astype(xe_ref.dtype)
        gs_ref[e, 0, win, :] += jnp.sum(jnp.where(hots[e], aff_ref[0, e:e + 1, :], 0.0), axis=1, keepdims=True)

    @pl.when(most_windows > 1)
    def _():
        for e in range(group):
            start, n_windows = plans[e]

            def more(k, carry, e=e, start=start):
                lo, st = _later_window(start, k, capacity)
                slots = slot_ref[0, e:e + 1, :]
                hot = jnp.logical_and(slots - st == w_iota, slots >= lo)
                extra = jnp.dot(jnp.where(hot, 1.0, 0.0).astype(BF16), h, preferred_element_type=F32)
                xe_ref[e, 0, pl.ds(st, ROUTE_WINDOW), :] += extra.astype(xe_ref.dtype)
                gs_ref[e, 0, pl.ds(st, ROUTE_WINDOW), :] += jnp.sum(
                    jnp.where(hot, aff_ref[0, e:e + 1, :], 0.0), axis=1, keepdims=True)
                return carry

            lax.fori_loop(1, n_windows, more, 0)


def _dispatch(h2, slot, aff, base_flat, capacity):
    bsz, s, d = h2.shape
    n_exp = slot.shape[1]
    chunk = min(ROUTE_CHUNK, s)
    group = min(EXPERT_GROUP, n_exp)
    n_chunks = s // chunk
    return pl.pallas_call(
        functools.partial(_dispatch_kernel, n_exp=n_exp, n_chunks=n_chunks, capacity=capacity),
        grid_spec=pltpu.PrefetchScalarGridSpec(
            num_scalar_prefetch=1,
            grid=(bsz, n_exp // group, n_chunks),
            in_specs=[pl.BlockSpec((1, chunk, d), lambda b, g, c, base: (b, c, 0)),
                      pl.BlockSpec((1, group, chunk), lambda b, g, c, base: (b, g, c)),
                      pl.BlockSpec((1, group, chunk), lambda b, g, c, base: (b, g, c))],
            out_specs=[pl.BlockSpec((group, 1, capacity, d), lambda b, g, c, base: (g, b, 0, 0)),
                       pl.BlockSpec((group, 1, capacity, 1), lambda b, g, c, base: (g, b, 0, 0))]),
        out_shape=[jax.ShapeDtypeStruct((n_exp, bsz, capacity, d), BF16),
                   jax.ShapeDtypeStruct((n_exp, bsz, capacity, 1), F32)],
        compiler_params=_params("arbitrary", "arbitrary", "arbitrary"),
        name="dispatch",
    )(base_flat, h2, slot, aff)


def _expert_kernel(x_ref, g_ref, wg_ref, wu_ref, wd_ref, y_ref):
    x = x_ref[0, 0]
    gate = jnp.dot(x, wg_ref[0], preferred_element_type=F32)
    up = jnp.dot(x, wu_ref[0], preferred_element_type=F32)
    he = (gate * _sigmoid(gate) * up).astype(BF16)
    y_ref[0, 0] = (jnp.dot(he, wd_ref[0], preferred_element_type=F32) * g_ref[0, 0]).astype(y_ref.dtype)


def _experts(xe, gates, w_gate, w_up, w_down):
    n_exp, bsz, cap, d = xe.shape
    ff = w_gate.shape[2]
    row = lambda e, i: (e, i, 0, 0)
    per_expert = lambda e, i: (e, 0, 0)
    return pl.pallas_call(
        _expert_kernel,
        grid=(n_exp, bsz),
        in_specs=[pl.BlockSpec((1, 1, cap, d), row), pl.BlockSpec((1, 1, cap, 1), row),
                  pl.BlockSpec((1, d, ff), per_expert), pl.BlockSpec((1, d, ff), per_expert),
                  pl.BlockSpec((1, ff, d), per_expert)],
        out_specs=pl.BlockSpec((1, 1, cap, d), row),
        out_shape=jax.ShapeDtypeStruct(xe.shape, BF16),
        compiler_params=_params("arbitrary", "arbitrary"),
        name="experts",
    )(xe, gates, w_gate.astype(BF16), w_up.astype(BF16), w_down.astype(BF16))


def _combine_kernel(base_ref, y_ref, slot_ref, x1_ref, gate_ref, out_ref, acc_s, *, n_chunks, capacity):
    b, c = pl.program_id(0), pl.program_id(1)
    n_exp = y_ref.shape[0]
    chunk = x1_ref.shape[1]
    w_iota = lax.broadcasted_iota(jnp.int32, (ROUTE_WINDOW, chunk), 0)
    plans, hots, wins = [], [], []
    for e in range(n_exp):
        start, n_windows = _window_plan(base_ref, (b * n_exp + e) * (n_chunks + 1), c, capacity)
        plans.append((start, n_windows))
        hots.append(jnp.where((slot_ref[0, e:e + 1, :] - start) == w_iota, 1.0, 0.0))
        wins.append(y_ref[e, 0, pl.ds(start, ROUTE_WINDOW), :])
    hot = jnp.concatenate(hots, axis=0).astype(BF16)
    ywin = jnp.concatenate(wins, axis=0)
    acc = lax.dot_general(hot, ywin, (((0,), (0,)), ((), ())), preferred_element_type=F32)
    most_windows = plans[0][1]
    for _, n_windows in plans[1:]:
        most_windows = jnp.maximum(most_windows, n_windows)

    @pl.when(most_windows <= 1)
    def _():
        out_ref[0] = x1_ref[0] + gate_ref[0] * acc

    @pl.when(most_windows > 1)
    def _():
        acc_s[...] = acc
        for e in range(n_exp):
            start, n_windows = plans[e]

            def more(k, carry, e=e, start=start):
                lo, st = _later_window(start, k, capacity)
                slots = slot_ref[0, e:e + 1, :]
                sel = jnp.logical_and(slots - st == w_iota, slots >= lo)
                acc_s[...] += lax.dot_general(jnp.where(sel, 1.0, 0.0).astype(BF16),
                                              y_ref[e, 0, pl.ds(st, ROUTE_WINDOW), :],
                                              (((0,), (0,)), ((), ())), preferred_element_type=F32)
                return carry

            lax.fori_loop(1, n_windows, more, 0)
        out_ref[0] = x1_ref[0] + gate_ref[0] * acc_s[...]


def _combine(ys, slot, base_flat, x1, gate2, capacity):
    bsz, s, d = x1.shape
    n_exp = ys.shape[0]
    chunk = min(ROUTE_CHUNK, s)
    n_chunks = s // chunk
    return pl.pallas_call(
        functools.partial(_combine_kernel, n_chunks=n_chunks, capacity=capacity),
        grid_spec=pltpu.PrefetchScalarGridSpec(
            num_scalar_prefetch=1,
            grid=(bsz, n_chunks),
            in_specs=[pl.BlockSpec((n_exp, 1, capacity, d), lambda b, c, base: (0, b, 0, 0)),
                      pl.BlockSpec((1, n_exp, chunk), lambda b, c, base: (b, 0, c)),
                      pl.BlockSpec((1, chunk, d), lambda b, c, base: (b, c, 0)),
                      pl.BlockSpec((1, 1, d), lambda b, c, base: (b, 0, 0))],
            out_specs=pl.BlockSpec((1, chunk, d), lambda b, c, base: (b, c, 0)),
            scratch_shapes=[pltpu.VMEM((chunk, d), F32)]),
        out_shape=jax.ShapeDtypeStruct(x1.shape, F32),
        compiler_params=_params("arbitrary", "arbitrary"),
        name="combine",
    )(base_flat, ys, slot, x1, gate2.reshape(bsz, 1, d))


def _layer(x, c, positions, w_ada, b_ada, norm1_w, w_in, q_norm_w, k_norm_w, conv_w, conv_b, lru_w_a, lru_b_a,
           lru_w_x, lru_b_x, lru_lambda, w_attn_branch, w_lru_branch, w_out, norm2_w, w_router, w_gate, w_up,
           w_down):
    bsz, s, d = x.shape
    mod = _ada(c, w_ada, b_ada)
    shift1, scale1, gate1, shift2, scale2, gate2 = jnp.split(mod, 6, axis=-1)

    n_qkv = 3 * ATTN_WIDTH
    gain1 = norm1_w * (1.0 + scale1)
    q0, q1, q2, k0, k1, k2, v0, v1, v2 = _inproj(x, positions, shift1, gain1, w_in[:, :n_qkv].astype(BF16),
                                                 q_norm_w, k_norm_w)
    col_scale = jnp.where(jnp.arange(w_in.shape[1] - n_qkv) >= 2 * d, 0.5, 1.0).astype(F32)
    xr, yr, ga, gl = _inproj_wide(x, shift1, gain1, (w_in[:, n_qkv:] * col_scale).astype(BF16))

    outs, lses = [], []
    for (window, dilation), q, k, v in zip(ATTN_GROUPS, (q0, q1, q2), (k0, k1, k2), (v0, v1, v2)):
        o, lse = _attention_group(q, k, v, window, dilation)
        outs.append(o)
        lses.append(lse)

    h_fwd = _lru_pass(xr, conv_w, conv_b, lru_w_a[0], lru_b_a[0], lru_w_x[0], lru_b_x[0], lru_lambda[0],
                      reverse=False)
    lru = _lru_pass(xr, conv_w, conv_b, lru_w_a[1], lru_b_a[1], lru_w_x[1], lru_b_x[1], lru_lambda[1],
                    reverse=True, h_fwd=h_fwd, yr=yr)

    x1, h2, aff = _merge(outs, lses, lru, ga, gl, x, gate1, w_attn_branch, w_lru_branch, w_out, shift2,
                         norm2_w * (1.0 + scale2), w_router)

    capacity = max(1, CAPACITY_FACTOR * s // N_EXPERTS)
    n_chunks = s // min(ROUTE_CHUNK, s)
    slot, base_table = _route(aff, capacity)
    base_flat = base_table[:, :, :n_chunks + 1].reshape(-1)
    xe, gates = _dispatch(h2, slot, aff, base_flat, capacity)
    ys = _experts(xe, gates, w_gate, w_up, w_down)
    return _combine(ys, slot, base_flat, x1, gate2, capacity)


def kernel(x, c, positions, w_ada, b_ada, norm1_w, w_in, q_norm_w, k_norm_w, conv_w, conv_b, lru_w_a, lru_b_a,
           lru_w_x, lru_b_x, lru_lambda, w_attn_branch, w_lru_branch, w_out, norm2_w, w_router, w_gate, w_up,
           w_down):
    for l in range(w_ada.shape[0]):
        x = _layer(x, c, positions, w_ada[l], b_ada[l], norm1_w[l], w_in[l], q_norm_w[l], k_norm_w[l],
                   conv_w[l, :, 0, :], conv_b[l], lru_w_a[l], lru_b_a[l], lru_w_x[l], lru_b_x[l], lru_lambda[l],
                   w_attn_branch[l], w_lru_branch[l], w_out[l], norm2_w[l], w_router[l], w_gate[l], w_up[l],
                   w_down[l])
    return x
```

```python
import functools

import jax
import jax.numpy as jnp
from jax import lax
from jax.experimental import pallas as pl
from jax.experimental.pallas import tpu as pltpu

F32 = jnp.float32
BF16 = jnp.bfloat16

HEAD_DIM = 64
HEADS_PER_GROUP = 4
GROUP_WIDTH = HEADS_PER_GROUP * HEAD_DIM
ATTN_GROUPS = ((128, 1), (512, 4), (2048, 16))
ATTN_WIDTH = GROUP_WIDTH * len(ATTN_GROUPS)
LRU_C = 8.0
CONV_WIDTH = 4
N_EXPERTS = 16
CAPACITY_FACTOR = 2
ROPE_THETA = 10000.0
NORM_EPS = 1e-6
NEG_INF = -1e30
GELU_C0 = 0.7978845608028654
GELU_C1 = 0.044715
LOG2_E = 1.4426950408889634
TINY = 1e-30

LANES = 128
MXU_DIM = 256
VMEM_LIMIT = 56 * 1024 * 1024

ROW_TILE = 512
WIDE_ROW_TILE = 1024
ATTN_TQ = 128
ATTN_CLASSES_PER_STEP = 4
LRU_TC = 128
LRU_NB = 8
HALO = 16
ROUTE_CHUNK = 256
ROUTE_WINDOW = 64
ROUTE_ALIGN = 16
EXPERT_GROUP = 16
ROPE_PACK = LANES // (HEAD_DIM // 2)


def _sigmoid(t):
    return 0.5 * jnp.tanh(0.5 * t) + 0.5


def _params(*sem):
    return pltpu.CompilerParams(dimension_semantics=sem, vmem_limit_bytes=VMEM_LIMIT)


def _ada_kernel(c_ref, w_ref, b_ref, o_ref):
    c = c_ref[...]
    o_ref[...] = jnp.dot(c * _sigmoid(c), w_ref[...], preferred_element_type=F32,
                         precision=lax.Precision.HIGHEST) + b_ref[...]


def _ada(c, w, b):
    bsz, d = c.shape
    n = w.shape[1]
    tn = n // 4
    return pl.pallas_call(
        _ada_kernel,
        grid=(n // tn,),
        in_specs=[pl.BlockSpec((bsz, d), lambda j: (0, 0)),
                  pl.BlockSpec((d, tn), lambda j: (0, j)),
                  pl.BlockSpec((1, tn), lambda j: (0, j))],
        out_specs=pl.BlockSpec((bsz, tn), lambda j: (0, j)),
        out_shape=jax.ShapeDtypeStruct((bsz, n), F32),
        compiler_params=_params("arbitrary"),
        name="ada",
    )(c, w, b.reshape(1, n))


def _store_classes(val, out_ref, relayout_s, dilation):
    if dilation == 1:
        out_ref[0] = val.astype(out_ref.dtype)
        return
    n_tiles = GROUP_WIDTH // LANES
    rows = val.shape[0] // dilation
    for t in range(n_tiles):
        relayout_s[t] = val[:, t * LANES:(t + 1) * LANES]
    for r in range(dilation):
        for t in range(n_tiles):
            lo = r * GROUP_WIDTH + t * LANES
            out_ref[0, :, lo:lo + LANES] = relayout_s[t, pl.ds(r, rows, stride=dilation), :].astype(out_ref.dtype)


def _inproj_kernel(x_ref, pos_ref, shift_ref, gain_ref, w_ref, qnw_ref, knw_ref, invf_ref, seg_ref,
                   spread_ref, q0_ref, q1_ref, q2_ref, k0_ref, k1_ref, k2_ref, v0_ref, v1_ref, v2_ref,
                   relayout_s, cos_s, sin_s):
    tm = x_ref.shape[1]
    lane = lax.broadcasted_iota(jnp.int32, (tm, GROUP_WIDTH), 1)
    first_half = (lane & (HEAD_DIM // 2)) == 0
    seg = seg_ref[...]

    x = x_ref[0]
    ms = jnp.mean(x * x, axis=-1, keepdims=True)
    hb = (x * lax.rsqrt(ms + NORM_EPS) * gain_ref[0] + shift_ref[0]).astype(BF16)

    packed_rows = tm // ROPE_PACK
    ang = pos_ref[0].astype(F32) * invf_ref[...]
    cos_p = jnp.cos(ang)
    sin_p = jnp.sin(ang)
    for p in range(ROPE_PACK):
        rows_p = pl.ds(p, packed_rows, stride=ROPE_PACK)
        cos_s[rows_p, :] = jnp.dot(cos_p, spread_ref[p], preferred_element_type=F32,
                                   precision=lax.Precision.HIGHEST)
        sin_s[rows_p, :] = jnp.dot(sin_p, spread_ref[p], preferred_element_type=F32,
                                   precision=lax.Precision.HIGHEST)
    cos = cos_s[...]
    sin = sin_s[...]
    cos2 = jnp.concatenate([cos, cos], axis=1)
    sin2 = jnp.concatenate([sin, sin], axis=1)
    sin2 = jnp.where(first_half, -sin2, sin2)

    def project(off, j):
        lo_c = off + j * GROUP_WIDTH
        t = jnp.dot(hb, w_ref[:, lo_c:lo_c + GROUP_WIDTH], preferred_element_type=F32)
        return t, (t * t).astype(BF16)

    def norm_rope(projected, w_norm_ref, out_ref, j):
        t, squares = projected
        lo_c = j * GROUP_WIDTH
        msq = jnp.dot(squares, seg, preferred_element_type=F32) * (1.0 / HEAD_DIM)
        y = t * lax.rsqrt(msq + NORM_EPS) * w_norm_ref[:, lo_c:lo_c + GROUP_WIDTH]
        rot = jnp.where(first_half, pltpu.roll(y, GROUP_WIDTH - HEAD_DIM // 2, 1),
                        pltpu.roll(y, HEAD_DIM // 2, 1))
        _store_classes(y * cos2 + rot * sin2, out_ref, relayout_s, ATTN_GROUPS[j][1])

    def value(out_ref, j):
        off = 2 * ATTN_WIDTH + j * GROUP_WIDTH
        v = jnp.dot(hb, w_ref[:, off:off + GROUP_WIDTH], preferred_element_type=F32)
        _store_classes(v, out_ref, relayout_s, ATTN_GROUPS[j][1])

    tiles = [(0, qnw_ref, q0_ref, 0), (ATTN_WIDTH, knw_ref, k0_ref, 0), (0, qnw_ref, q1_ref, 1),
             (ATTN_WIDTH, knw_ref, k1_ref, 1), (0, qnw_ref, q2_ref, 2), (ATTN_WIDTH, knw_ref, k2_ref, 2)]
    values = [(v0_ref, 0), (v1_ref, 1), (v2_ref, 2)]
    ahead = project(tiles[0][0], tiles[0][3])
    for n, (off, w_norm_ref, out_ref, j) in enumerate(tiles):
        current = ahead
        if n + 1 < len(tiles):
            ahead = project(tiles[n + 1][0], tiles[n + 1][3])
        if n < len(values):
            value(*values[n])
        norm_rope(current, w_norm_ref, out_ref, j)


def _inproj_wide_kernel(x_ref, shift_ref, gain_ref, w_ref, *out_refs):
    x = x_ref[0]
    d = x.shape[1]
    ms = jnp.mean(x * x, axis=-1, keepdims=True)
    hb = (x * lax.rsqrt(ms + NORM_EPS) * gain_ref[0] + shift_ref[0]).astype(BF16)
    for i, out_ref in enumerate(out_refs):
        out_ref[0] = jnp.dot(hb, w_ref[:, i * d:(i + 1) * d], preferred_element_type=F32).astype(out_ref.dtype)


def _inproj_wide(x, shift, gain, w_wide):
    bsz, s, d = x.shape
    n_out = w_wide.shape[1] // d
    tm = min(WIDE_ROW_TILE, s)
    row = lambda b, i: (b, i, 0)
    per_batch = lambda b, i: (b, 0, 0)
    const = lambda b, i: (0, 0)
    return pl.pallas_call(
        _inproj_wide_kernel,
        grid=(bsz, s // tm),
        in_specs=[pl.BlockSpec((1, tm, d), row),
                  pl.BlockSpec((1, 1, d), per_batch),
                  pl.BlockSpec((1, 1, d), per_batch),
                  pl.BlockSpec(w_wide.shape, const, pipeline_mode=pl.Buffered(1))],
        out_specs=[pl.BlockSpec((1, tm, d), row)] * n_out,
        out_shape=[jax.ShapeDtypeStruct((bsz, s, d), BF16)] * n_out,
        compiler_params=_params("arbitrary", "arbitrary"),
        name="inproj_wide",
    )(x, shift.reshape(bsz, 1, d), gain.reshape(bsz, 1, d), w_wide)


def _inproj(x, positions, shift, gain, w_in, q_norm_w, k_norm_w):
    bsz, s, d = x.shape
    tm = min(WIDE_ROW_TILE, s)
    half = HEAD_DIM // 2
    inv_freq = ROPE_THETA ** (-jnp.arange(half, dtype=F32) / half)
    inv_freq = jnp.tile(inv_freq, ROPE_PACK).reshape(1, LANES)
    pos_packed = jnp.repeat(positions.reshape(bsz, s // ROPE_PACK, ROPE_PACK), half, axis=-1)
    lane = jnp.arange(LANES)
    spread = (lane[None, :, None] == (jnp.arange(ROPE_PACK)[:, None, None] * half + lane[None, None, :] % half))
    spread = spread.astype(F32)
    head_of = jnp.arange(GROUP_WIDTH) // HEAD_DIM
    seg = (head_of[:, None] == head_of[None, :]).astype(BF16)
    row = lambda b, i: (b, i, 0)
    per_batch = lambda b, i: (b, 0, 0)
    const = lambda b, i: (0, 0)
    narrow = [jax.ShapeDtypeStruct((bsz, s // dil, dil * GROUP_WIDTH), BF16) for _, dil in ATTN_GROUPS] * 3
    narrow_specs = [pl.BlockSpec((1, tm // dil, dil * GROUP_WIDTH), row) for _, dil in ATTN_GROUPS] * 3
    return pl.pallas_call(
        _inproj_kernel,
        grid=(bsz, s // tm),
        in_specs=[pl.BlockSpec((1, tm, d), row),
                  pl.BlockSpec((1, tm // ROPE_PACK, LANES), row),
                  pl.BlockSpec((1, 1, d), per_batch),
                  pl.BlockSpec((1, 1, d), per_batch),
                  pl.BlockSpec(w_in.shape, const),
                  pl.BlockSpec((1, ATTN_WIDTH), const),
                  pl.BlockSpec((1, ATTN_WIDTH), const),
                  pl.BlockSpec((1, LANES), const),
                  pl.BlockSpec((GROUP_WIDTH, GROUP_WIDTH), const),
                  pl.BlockSpec(spread.shape, lambda b, i: (0, 0, 0))],
        out_specs=narrow_specs,
        out_shape=narrow,
        scratch_shapes=[pltpu.VMEM((GROUP_WIDTH // LANES, tm, LANES), F32),
                        pltpu.VMEM((tm, LANES), F32), pltpu.VMEM((tm, LANES), F32)],
        compiler_params=_params("arbitrary", "arbitrary"),
        name="inproj",
    )(x, pos_packed, shift.reshape(bsz, 1, d), gain.reshape(bsz, 1, d), w_in,
      (q_norm_w * HEAD_DIM ** -0.5).reshape(1, ATTN_WIDTH),
      k_norm_w.reshape(1, ATTN_WIDTH), inv_freq, seg, spread)


def _attn_kernel(q_ref, k_ref, v_ref, o_ref, lse_ref, *, length, tq, tk, half):
    rel0 = (lax.broadcasted_iota(jnp.int32, (2 * tq, tk), 1)
            - (lax.broadcasted_iota(jnp.int32, (2 * tq, tk), 0) & (tq - 1)))
    lane = lax.broadcasted_iota(jnp.int32, (tq, LANES), 1)
    head_a = lane < HEAD_DIM
    n_tiles = length // tq

    per_trip = 4 if n_tiles % 4 == 0 else (2 if n_tiles % 2 == 0 else 1)

    def scores(item):
        t0, ws, _, cols = item
        qp = q_ref[0, pl.ds(t0, tq), cols]
        kp = k_ref[0, pl.ds(ws, tk), cols]
        zero = jnp.zeros_like(qp)
        q2 = jnp.concatenate([jnp.where(head_a, qp, zero), jnp.where(head_a, zero, qp)], axis=0)
        return lax.dot_general(q2, kp, (((1,), (1,)), ((), ())), preferred_element_type=F32)

    def trip(g, carry, col0):
        items = []
        for u in range(per_trip):
            t0 = pl.multiple_of((g * per_trip + u) * tq, tq)
            ws = pl.multiple_of(jnp.clip(t0 - half, 0, length - tk), half)
            valid = jnp.abs(rel0 + (ws - t0)) <= half
            for p in range(GROUP_WIDTH // LANES):
                items.append((t0, ws, valid, slice(col0 + p * LANES, col0 + (p + 1) * LANES)))
        ahead = scores(items[0])
        for n, (t0, ws, valid, cols) in enumerate(items):
            s = ahead
            if n + 1 < len(items):
                ahead = scores(items[n + 1])
            s = jnp.where(valid, s, NEG_INF)
            m = jnp.max(s, axis=1, keepdims=True)
            e = jnp.exp(s - m)
            den = jnp.sum(e, axis=1, keepdims=True)
            vp = v_ref[0, pl.ds(ws, tk), cols]
            o = jnp.dot(e.astype(BF16), vp, preferred_element_type=F32) * (1.0 / den)
            lse = jnp.broadcast_to(m + jnp.log(den), (2 * tq, LANES))
            o_ref[0, pl.ds(t0, tq), cols] = jnp.where(head_a, o[:tq], o[tq:]).astype(o_ref.dtype)
            lse_ref[0, pl.ds(t0, tq), cols] = jnp.where(head_a, lse[:tq], lse[tq:])
        return carry

    for cls in range(q_ref.shape[2] // GROUP_WIDTH):
        lax.fori_loop(0, n_tiles // per_trip, functools.partial(trip, col0=cls * GROUP_WIDTH), 0)


def _attention_group(q, k, v, window, dilation):
    bsz, length, _ = q.shape
    half = window // (2 * dilation)
    tq = min(ATTN_TQ, length)
    assert tq & (tq - 1) == 0, "query tile must be a power of two"
    tk = min(tq + 2 * half, length)
    classes = min(dilation, ATTN_CLASSES_PER_STEP)
    spec = pl.BlockSpec((1, length, classes * GROUP_WIDTH), lambda b, r: (b, 0, r))
    return pl.pallas_call(
        functools.partial(_attn_kernel, length=length, tq=tq, tk=tk, half=half),
        grid=(bsz, dilation // classes),
        in_specs=[spec] * 3,
        out_specs=[spec] * 2,
        out_shape=[jax.ShapeDtypeStruct(q.shape, BF16), jax.ShapeDtypeStruct(q.shape, F32)],
        compiler_params=_params("arbitrary", "arbitrary"),
        name=f"attn_d{dilation}",
    )(q, k, v)


def _gelu_tanh(t):
    return 0.5 * t * (1.0 + jnp.tanh(GELU_C0 * (t + GELU_C1 * t * t * t)))


def _lru_kernel(*refs, reverse, final, n_chunks):
    if final:
        (xr_ref, xp_ref, xn_ref, hf_ref, yr_ref, shift_ref, cw_ref, cb_ref, wa_ref, wx_ref, ba_ref, bx_ref, lam_ref,
         out_ref, a_s, u_s, h_s, state_s) = refs
    else:
        (xr_ref, xp_ref, xn_ref, shift_ref, cw_ref, cb_ref, wa_ref, wx_ref, ba_ref, bx_ref, lam_ref,
         out_ref, a_s, u_s, h_s, state_s) = refs
    nb, tc, ch = xr_ref.shape
    n_slabs = ch // LANES
    step = pl.program_id(1)
    chunk = (n_chunks - 1 - step) if reverse else step

    @pl.when(step == 0)
    def _():
        state_s[...] = jnp.zeros_like(state_s)

    lam = lam_ref[...]
    neg_c_softplus = -LRU_C * (jnp.maximum(-lam, 0.0) + jnp.log(1.0 + jnp.exp(-jnp.abs(lam))))
    half_log2_decay = (0.5 * LOG2_E) * neg_c_softplus
    pad_rows = jnp.zeros((shift_ref.shape[1] - tc - 2 * HALO, ch), BF16)

    rows_per_trip = 4 if nb % 4 == 0 else (2 if nb % 2 == 0 else 1)
    n_tiles = ch // MXU_DIM
    tile_cols = [slice(kk * MXU_DIM, (kk + 1) * MXU_DIM) for kk in range(n_tiles)]
    half_cw = 0.5 * cw_ref[...]
    half_cb = 0.5 * cb_ref[...]

    def coeffs(trip, carry):
        batch_rows = [trip * rows_per_trip + r for r in range(rows_per_trip)]
        xcs = []
        for b in batch_rows:
            before = jnp.where(chunk > 0, xp_ref[b], jnp.zeros((HALO, ch), BF16))
            after = jnp.where(chunk < n_chunks - 1, xn_ref[b], jnp.zeros((HALO, ch), BF16))
            stack = jnp.concatenate([before, xr_ref[b], after, pad_rows], axis=0)
            for cols in tile_cols:
                taps = jnp.dot(shift_ref[...], stack[:, cols], preferred_element_type=F32)
                half_x = half_cb[:, cols]
                for j in range(CONV_WIDTH):
                    half_x = half_x + half_cw[j:j + 1, cols] * taps[j * tc:(j + 1) * tc]
                xcs.append(half_x)
        pre = []
        for n, half_x in enumerate(xcs):
            kk = n % n_tiles
            xcb = half_x.astype(BF16)
            pre.append((jnp.dot(xcb, wa_ref[kk], preferred_element_type=F32),
                        jnp.dot(xcb, wx_ref[kk], preferred_element_type=F32)))
        for n, half_x in enumerate(xcs):
            kk, cols = n % n_tiles, tile_cols[n % n_tiles]
            slab_rows = pl.ds(batch_rows[n // n_tiles], tc, stride=nb)
            tanh_r = jnp.tanh(pre[n][0] + ba_ref[:, cols])
            tanh_i = jnp.tanh(pre[n][1] + bx_ref[:, cols])
            a = jnp.exp2(tanh_r * half_log2_decay[:, cols] + half_log2_decay[:, cols])
            v = 1.0 - a * a
            root = v * lax.rsqrt(jnp.maximum(v, TINY))
            u = root * (tanh_i * half_x + half_x)
            for t in range(MXU_DIM // LANES):
                j = kk * (MXU_DIM // LANES) + t
                a_s[j, slab_rows, :] = a[:, t * LANES:(t + 1) * LANES]
                u_s[j, slab_rows, :] = u[:, t * LANES:(t + 1) * LANES]
        return carry

    lax.fori_loop(0, nb // rows_per_trip, coeffs, 0)

    def scan_step(i, hs):
        t = (tc - 1 - i) if reverse else i
        new = []
        for j in range(n_slabs):
            rows = pl.ds(pl.multiple_of(t * nb, nb), nb)
            hj = a_s[j, rows, :] * hs[j] + u_s[j, rows, :]
            h_s[j, rows, :] = hj
            new.append(hj)
        return tuple(new)

    hs = lax.fori_loop(0, tc, scan_step, tuple(state_s[j] for j in range(n_slabs)), unroll=2)
    for j in range(n_slabs):
        state_s[j] = hs[j]

    def emit(b, carry):
        hb = jnp.concatenate([h_s[j, pl.ds(b, tc, stride=nb), :] for j in range(n_slabs)], axis=1)
        if final:
            hb = (hb + hf_ref[b].astype(F32)) * _gelu_tanh(yr_ref[b].astype(F32))
        out_ref[b] = hb.astype(out_ref.dtype)
        return carry

    lax.fori_loop(0, nb, emit, 0, unroll=2 if nb % 2 == 0 else 1)


def _block_diag_tiles(w):
    per_tile = MXU_DIM // w.shape[1]
    n_tiles = w.shape[0] // per_tile
    w = w.reshape(n_tiles, per_tile, w.shape[1], w.shape[2])
    eye = jnp.eye(per_tile, dtype=w.dtype)
    tiles = jnp.einsum('tpkj,pq->tpkqj', w, eye)
    return tiles.reshape(n_tiles, MXU_DIM, MXU_DIM).astype(BF16)


def _lru_pass(xr, conv_w, conv_b, w_a, b_a, w_x, b_x, lam, *, reverse, h_fwd=None, yr=None):
    bsz, s, ch = xr.shape
    nb = min(LRU_NB, bsz)
    tc = min(LRU_TC, s)
    n_chunks = s // tc
    final = h_fwd is not None
    chunk_of = (lambda c: n_chunks - 1 - c) if reverse else (lambda c: c)
    per_halo = tc // HALO
    last_halo = s // HALO - 1
    main = pl.BlockSpec((nb, tc, ch), lambda g, c: (g, chunk_of(c), 0))
    prev = pl.BlockSpec((nb, HALO, ch), lambda g, c: (g, jnp.maximum(chunk_of(c) * per_halo - 1, 0), 0))
    nxt = pl.BlockSpec((nb, HALO, ch), lambda g, c: (g, jnp.minimum((chunk_of(c) + 1) * per_halo, last_halo), 0))
    const2 = lambda g, c: (0, 0)
    const3 = lambda g, c: (0, 0, 0)
    vec = pl.BlockSpec((1, ch), const2)
    tiles = pl.BlockSpec((ch // MXU_DIM, MXU_DIM, MXU_DIM), const3)
    acts = [xr, xr, xr] + ([h_fwd, yr] if final else [])
    act_specs = [main, prev, nxt] + ([main, main] if final else [])
    slab = pltpu.VMEM((ch // LANES, tc * nb, LANES), F32)
    stack_rows = -(-(tc + 2 * HALO) // MXU_DIM) * MXU_DIM
    tap_row = HALO - CONV_WIDTH // 2 + jnp.arange(CONV_WIDTH)[:, None] + jnp.arange(tc)[None, :]
    shifts = (tap_row.reshape(-1, 1) == jnp.arange(stack_rows)[None, :]).astype(BF16)
    return pl.pallas_call(
        functools.partial(_lru_kernel, reverse=reverse, final=final, n_chunks=n_chunks),
        grid=(bsz // nb, n_chunks),
        in_specs=act_specs + [pl.BlockSpec(shifts.shape, const2), pl.BlockSpec((CONV_WIDTH, ch), const2), vec,
                              tiles, tiles, vec, vec, vec],
        out_specs=main,
        out_shape=jax.ShapeDtypeStruct((bsz, s, ch), BF16),
        scratch_shapes=[slab, slab, slab, pltpu.VMEM((ch // LANES, nb, LANES), F32)],
        compiler_params=_params("arbitrary", "arbitrary"),
        name="lru_bwd" if reverse else "lru_fwd",
    )(*acts, shifts, conv_w.reshape(CONV_WIDTH, ch), conv_b.reshape(1, ch), _block_diag_tiles(w_a),
      _block_diag_tiles(w_x), (0.5 * b_a).reshape(1, ch), (0.5 * b_x).reshape(1, ch), lam.reshape(1, ch))


def _load_classes(in_ref, relayout_s, dilation):
    if dilation == 1:
        return in_ref[0].astype(F32)
    n_tiles = GROUP_WIDTH // LANES
    rows = in_ref.shape[1]
    for r in range(dilation):
        for t in range(n_tiles):
            lo = r * GROUP_WIDTH + t * LANES
            relayout_s[t, pl.ds(r, rows, stride=dilation), :] = in_ref[0, :, lo:lo + LANES].astype(F32)
    return jnp.concatenate([relayout_s[t] for t in range(n_tiles)], axis=1)


def _merge_kernel(o0_ref, o1_ref, o2_ref, l0_ref, l1_ref, l2_ref, lru_ref, ga_ref, gl_ref, x_ref, gate_ref,
                  wab_ref, wlb_ref, wout_ref, shift_ref, gain_ref, wr_ref,
                  x1_ref, h2_ref, aff_ref, so1_s, so2_s, sl1_s, sl2_s):
    dil = [dilation for _, dilation in ATTN_GROUPS]
    l0 = _load_classes(l0_ref, None, dil[0])
    l1 = _load_classes(l1_ref, sl1_s, dil[1])
    l2 = _load_classes(l2_ref, sl2_s, dil[2])
    o0 = _load_classes(o0_ref, None, dil[0])
    o1 = _load_classes(o1_ref, so1_s, dil[1])
    o2 = _load_classes(o2_ref, so2_s, dil[2])
    m = jnp.maximum(jnp.maximum(l0, l1), l2)
    e0, e1, e2 = jnp.exp(l0 - m), jnp.exp(l1 - m), jnp.exp(l2 - m)
    attn = (e0 * o0 + e1 * o1 + e2 * o2) / (e0 + e1 + e2)
    branch_a = jnp.dot(attn.astype(BF16), wab_ref[...], preferred_element_type=F32)
    branch_l = jnp.dot(lru_ref[0], wlb_ref[...], preferred_element_type=F32)
    merged = ((jnp.tanh(ga_ref[0].astype(F32)) + 1.0) * branch_a
              + (jnp.tanh(gl_ref[0].astype(F32)) + 1.0) * branch_l)
    mix = jnp.dot(merged.astype(BF16), wout_ref[...], preferred_element_type=F32)
    x1 = x_ref[0] + gate_ref[0] * mix
    x1_ref[0] = x1
    ms = jnp.mean(x1 * x1, axis=-1, keepdims=True)
    h2 = x1 * lax.rsqrt(ms + NORM_EPS) * gain_ref[0] + shift_ref[0]
    h2_ref[0] = h2.astype(h2_ref.dtype)
    logits = lax.dot_general(wr_ref[...], h2, (((1,), (1,)), ((), ())), preferred_element_type=F32,
                             precision=lax.Precision.HIGHEST)
    z = jnp.exp(logits - jnp.max(logits, axis=0, keepdims=True))
    aff_ref[0] = z / jnp.sum(z, axis=0, keepdims=True)


def _merge(outs, lses, lru, ga, gl, x, gate1, w_ab, w_lb, w_out, shift2, gain2, w_router):
    bsz, s, d = x.shape
    tm = min(ROW_TILE, s)
    n_exp = w_router.shape[1]
    row = lambda b, i: (b, i, 0)
    per_batch = lambda b, i: (b, 0, 0)
    const = lambda b, i: (0, 0)
    grp = [pl.BlockSpec((1, tm // dil, dil * GROUP_WIDTH), row) for _, dil in ATTN_GROUPS]
    wide = pl.BlockSpec((1, tm, d), row)
    mod = pl.BlockSpec((1, 1, d), per_batch)
    full = lambda a: pl.BlockSpec(a.shape, const)
    w_ab, w_lb, w_out = (0.5 * w_ab).astype(BF16), (0.5 * w_lb).astype(BF16), w_out.astype(BF16)
    w_rt = w_router.T
    return pl.pallas_call(
        _merge_kernel,
        grid=(bsz, s // tm),
        in_specs=grp * 2 + [wide] * 4 + [mod, full(w_ab), full(w_lb), full(w_out), mod, mod, full(w_rt)],
        out_specs=[wide, wide, pl.BlockSpec((1, n_exp, tm), lambda b, i: (b, 0, i))],
        out_shape=[jax.ShapeDtypeStruct((bsz, s, d), F32), jax.ShapeDtypeStruct((bsz, s, d), BF16),
                   jax.ShapeDtypeStruct((bsz, n_exp, s), F32)],
        scratch_shapes=[pltpu.VMEM((GROUP_WIDTH // LANES, tm, LANES), F32)] * 4,
        compiler_params=_params("arbitrary", "arbitrary"),
        name="merge",
    )(*outs, *lses, lru, ga, gl, x, gate1.reshape(bsz, 1, d), w_ab, w_lb, w_out, shift2.reshape(bsz, 1, d),
      gain2.reshape(bsz, 1, d), w_rt)


def _prefix_counts(flags, strict_upper):
    n_exp, s = flags.shape
    n_tiles = s // MXU_DIM
    stacked = jnp.concatenate([flags[:, k * MXU_DIM:(k + 1) * MXU_DIM] for k in range(n_tiles)], axis=0)
    within = jnp.dot(stacked.astype(BF16), strict_upper, preferred_element_type=F32)
    totals = jnp.sum(stacked, axis=1, keepdims=True)
    run = jnp.zeros((n_exp, 1), F32)
    pieces, bases = [], []
    for k in range(n_tiles):
        bases.append(run)
        pieces.append(within[k * n_exp:(k + 1) * n_exp] + run)
        run = run + totals[k * n_exp:(k + 1) * n_exp]
    bases.append(run)
    return jnp.concatenate(pieces, axis=1), bases


def _route_kernel(aff_ref, upper_ref, slot_ref, base_ref, *, capacity):
    aff = aff_ref[0]
    cap = jnp.float32(capacity)

    def refine(i, thr_bits):
        cand = thr_bits | jnp.left_shift(jnp.int32(1), 30 - i)
        cnt = jnp.sum(jnp.where(aff >= pltpu.bitcast(cand, F32), 1.0, 0.0), axis=1, keepdims=True)
        return jnp.where(cnt >= cap, cand, thr_bits)

    thr_bits = lax.fori_loop(0, 31, refine, jnp.zeros((aff.shape[0], 1), jnp.int32))
    thr = pltpu.bitcast(thr_bits, F32)
    above = jnp.where(aff > thr, 1.0, 0.0)
    tied = jnp.where(aff == thr, 1.0, 0.0)
    need = cap - jnp.sum(above, axis=1, keepdims=True)
    upper = upper_ref[...]
    tie_rank, _ = _prefix_counts(tied, upper)
    chosen = above + tied * jnp.where(tie_rank < need, 1.0, 0.0)
    slot, bases = _prefix_counts(chosen, upper)
    slot_ref[0] = jnp.where(chosen > 0.0, slot, -1.0).astype(jnp.int32)
    lane = lax.broadcasted_iota(jnp.int32, (aff.shape[0], LANES), 1)
    table = jnp.zeros((aff.shape[0], LANES), F32)
    for k, bk in enumerate(bases):
        table = jnp.where(lane == k, bk, table)
    base_ref[0] = table.astype(jnp.int32)


def _route(aff, capacity):
    bsz, n_exp, s = aff.shape
    idx = jnp.arange(MXU_DIM)
    upper = (idx[:, None] < idx[None, :]).astype(BF16)
    per_batch = lambda b: (b, 0, 0)
    return pl.pallas_call(
        functools.partial(_route_kernel, capacity=capacity),
        grid=(bsz,),
        in_specs=[pl.BlockSpec((1, n_exp, s), per_batch), pl.BlockSpec((MXU_DIM, MXU_DIM), lambda b: (0, 0))],
        out_specs=[pl.BlockSpec((1, n_exp, s), per_batch), pl.BlockSpec((1, n_exp, LANES), per_batch)],
        out_shape=[jax.ShapeDtypeStruct((bsz, n_exp, s), jnp.int32),
                   jax.ShapeDtypeStruct((bsz, n_exp, LANES), jnp.int32)],
        compiler_params=_params("arbitrary"),
        name="route",
    )(aff, upper)


def _window_plan(base_ref, row, chunk, capacity):
    first = base_ref[row + chunk]
    end = base_ref[row + chunk + 1]
    start = jnp.minimum((first // ROUTE_ALIGN) * ROUTE_ALIGN, capacity - ROUTE_WINDOW)
    n_windows = (end - start + ROUTE_WINDOW - 1) // ROUTE_WINDOW
    return pl.multiple_of(start, ROUTE_ALIGN), n_windows


def _later_window(start, k, capacity):
    lo = start + k * ROUTE_WINDOW
    return lo, pl.multiple_of(jnp.minimum(lo, capacity - ROUTE_WINDOW), ROUTE_ALIGN)


def _dispatch_kernel(base_ref, h_ref, slot_ref, aff_ref, xe_ref, gs_ref, *, n_exp, n_chunks, capacity):
    b, eg, c = pl.program_id(0), pl.program_id(1), pl.program_id(2)
    group = xe_ref.shape[0]
    chunk = h_ref.shape[1]

    @pl.when(c == 0)
    def _():
        xe_ref[...] = jnp.zeros_like(xe_ref)
        gs_ref[...] = jnp.zeros_like(gs_ref)

    h = h_ref[0]
    w_iota = lax.broadcasted_iota(jnp.int32, (ROUTE_WINDOW, chunk), 0)
    plans, hots = [], []
    for e in range(group):
        row = (b * n_exp + eg * group + e) * (n_chunks + 1)
        start, n_windows = _window_plan(base_ref, row, c, capacity)
        plans.append((start, n_windows))
        hots.append((slot_ref[0, e:e + 1, :] - start) == w_iota)
    stack = jnp.concatenate([jnp.where(hot, 1.0, 0.0) for hot in hots], axis=0).astype(BF16)
    rows = jnp.dot(stack, h, preferred_element_type=F32)
    most_windows = plans[0][1]
    for e in range(group):
        start, n_windows = plans[e]
        most_windows = jnp.maximum(most_windows, n_windows)
        win = pl.ds(start, ROUTE_WINDOW)
        xe_ref[e, 0, win, :] += rows[e * ROUTE_WINDOW:(e + 1) * ROUTE_WINDOW].astype(xe_ref.dtype)
        gs_ref[e, 0, win, :] += jnp.sum(jnp.where(hots[e], aff_ref[0, e:e + 1, :], 0.0), axis=1, keepdims=True)

    @pl.when(most_windows > 1)
    def _():
        for e in range(group):
            start, n_windows = plans[e]

            def more(k, carry, e=e, start=start):
                lo, st = _later_window(start, k, capacity)
                slots = slot_ref[0, e:e + 1, :]
                hot = jnp.logical_and(slots - st == w_iota, slots >= lo)
                extra = jnp.dot(jnp.where(hot, 1.0, 0.0).astype(BF16), h, preferred_element_type=F32)
                xe_ref[e, 0, pl.ds(st, ROUTE_WINDOW), :] += extra.astype(xe_ref.dtype)
                gs_ref[e, 0, pl.ds(st, ROUTE_WINDOW), :] += jnp.sum(
                    jnp.where(hot, aff_ref[0, e:e + 1, :], 0.0), axis=1, keepdims=True)
                return carry

            lax.fori_loop(1, n_windows, more, 0)


def _dispatch(h2, slot, aff, base_flat, capacity):
    bsz, s, d = h2.shape
    n_exp = slot.shape[1]
    chunk = min(ROUTE_CHUNK, s)
    group = min(EXPERT_GROUP, n_exp)
    n_chunks = s // chunk
    return pl.pallas_call(
        functools.partial(_dispatch_kernel, n_exp=n_exp, n_chunks=n_chunks, capacity=capacity),
        grid_spec=pltpu.PrefetchScalarGridSpec(
            num_scalar_prefetch=1,
            grid=(bsz, n_exp // group, n_chunks),
            in_specs=[pl.BlockSpec((1, chunk, d), lambda b, g, c, base: (b, c, 0)),
                      pl.BlockSpec((1, group, chunk), lambda b, g, c, base: (b, g, c)),
                      pl.BlockSpec((1, group, chunk), lambda b, g, c, base: (b, g, c))],
            out_specs=[pl.BlockSpec((group, 1, capacity, d), lambda b, g, c, base: (g, b, 0, 0)),
                       pl.BlockSpec((group, 1, capacity, 1), lambda b, g, c, base: (g, b, 0, 0))]),
        out_shape=[jax.ShapeDtypeStruct((n_exp, bsz, capacity, d), BF16),
                   jax.ShapeDtypeStruct((n_exp, bsz, capacity, 1), F32)],
        compiler_params=_params("arbitrary", "arbitrary", "arbitrary"),
        name="dispatch",
    )(base_flat, h2, slot, aff)


def _expert_kernel(x_ref, g_ref, wg_ref, wu_ref, wd_ref, y_ref):
    x = x_ref[0, 0]
    gate = jnp.dot(x, wg_ref[0], preferred_element_type=F32)
    up = jnp.dot(x, wu_ref[0], preferred_element_type=F32)
    he = (gate * _sigmoid(gate) * up).astype(BF16)
    y_ref[0, 0] = (jnp.dot(he, wd_ref[0], preferred_element_type=F32) * g_ref[0, 0]).astype(y_ref.dtype)


def _experts(xe, gates, w_gate, w_up, w_down):
    n_exp, bsz, cap, d = xe.shape
    ff = w_gate.shape[2]
    row = lambda e, i: (e, i, 0, 0)
    per_expert = lambda e, i: (e, 0, 0)
    return pl.pallas_call(
        _expert_kernel,
        grid=(n_exp, bsz),
        in_specs=[pl.BlockSpec((1, 1, cap, d), row), pl.BlockSpec((1, 1, cap, 1), row),
                  pl.BlockSpec((1, d, ff), per_expert), pl.BlockSpec((1, d, ff), per_expert),
                  pl.BlockSpec((1, ff, d), per_expert)],
        out_specs=pl.BlockSpec((1, 1, cap, d), row),
        out_shape=jax.ShapeDtypeStruct(xe.shape, BF16),
        compiler_params=_params("arbitrary", "arbitrary"),
        name="experts",
    )(xe, gates, w_gate.astype(BF16), w_up.astype(BF16), w_down.astype(BF16))


def _combine_kernel(base_ref, y_ref, slot_ref, x1_ref, gate_ref, out_ref, acc_s, *, n_chunks, capacity):
    b, c = pl.program_id(0), pl.program_id(1)
    n_exp = y_ref.shape[0]
    chunk = x1_ref.shape[1]
    w_iota = lax.broadcasted_iota(jnp.int32, (ROUTE_WINDOW, chunk), 0)
    plans, hots, wins = [], [], []
    for e in range(n_exp):
        start, n_windows = _window_plan(base_ref, (b * n_exp + e) * (n_chunks + 1), c, capacity)
        plans.append((start, n_windows))
        hots.append(jnp.where((slot_ref[0, e:e + 1, :] - start) == w_iota, 1.0, 0.0))
        wins.append(y_ref[e, 0, pl.ds(start, ROUTE_WINDOW), :])
    hot = jnp.concatenate(hots, axis=0).astype(BF16)
    ywin = jnp.concatenate(wins, axis=0)
    acc = lax.dot_general(hot, ywin, (((0,), (0,)), ((), ())), preferred_element_type=F32)
    most_windows = plans[0][1]
    for _, n_windows in plans[1:]:
        most_windows = jnp.maximum(most_windows, n_windows)

    @pl.when(most_windows <= 1)
    def _():
        out_ref[0] = x1_ref[0] + gate_ref[0] * acc

    @pl.when(most_windows > 1)
    def _():
        acc_s[...] = acc
        for e in range(n_exp):
            start, n_windows = plans[e]

            def more(k, carry, e=e, start=start):
                lo, st = _later_window(start, k, capacity)
                slots = slot_ref[0, e:e + 1, :]
                sel = jnp.logical_and(slots - st == w_iota, slots >= lo)
                acc_s[...] += lax.dot_general(jnp.where(sel, 1.0, 0.0).astype(BF16),
                                              y_ref[e, 0, pl.ds(st, ROUTE_WINDOW), :],
                                              (((0,), (0,)), ((), ())), preferred_element_type=F32)
                return carry

            lax.fori_loop(1, n_windows, more, 0)
        out_ref[0] = x1_ref[0] + gate_ref[0] * acc_s[...]


def _combine(ys, slot, base_flat, x1, gate2, capacity):
    bsz, s, d = x1.shape
    n_exp = ys.shape[0]
    chunk = min(ROUTE_CHUNK, s)
    n_chunks = s // chunk
    return pl.pallas_call(
        functools.partial(_combine_kernel, n_chunks=n_chunks, capacity=capacity),
        grid_spec=pltpu.PrefetchScalarGridSpec(
            num_scalar_prefetch=1,
            grid=(bsz, n_chunks),
            in_specs=[pl.BlockSpec((n_exp, 1, capacity, d), lambda b, c, base: (0, b, 0, 0)),
                      pl.BlockSpec((1, n_exp, chunk), lambda b, c, base: (b, 0, c)),
                      pl.BlockSpec((1, chunk, d), lambda b, c, base: (b, c, 0)),
                      pl.BlockSpec((1, 1, d), lambda b, c, base: (b, 0, 0))],
            out_specs=pl.BlockSpec((1, chunk, d), lambda b, c, base: (b, c, 0)),
            scratch_shapes=[pltpu.VMEM((chunk, d), F32)]),
        out_shape=jax.ShapeDtypeStruct(x1.shape, F32),
        compiler_params=_params("arbitrary", "arbitrary"),
        name="combine",
    )(base_flat, ys, slot, x1, gate2.reshape(bsz, 1, d))


def _layer(x, c, positions, w_ada, b_ada, norm1_w, w_in, q_norm_w, k_norm_w, conv_w, conv_b, lru_w_a, lru_b_a,
           lru_w_x, lru_b_x, lru_lambda, w_attn_branch, w_lru_branch, w_out, norm2_w, w_router, w_gate, w_up,
           w_down):
    bsz, s, d = x.shape
    mod = _ada(c, w_ada, b_ada)
    shift1, scale1, gate1, shift2, scale2, gate2 = jnp.split(mod, 6, axis=-1)

    n_qkv = 3 * ATTN_WIDTH
    gain1 = norm1_w * (1.0 + scale1)
    q0, q1, q2, k0, k1, k2, v0, v1, v2 = _inproj(x, positions, shift1, gain1, w_in[:, :n_qkv].astype(BF16),
                                                 q_norm_w, k_norm_w)
    col_scale = jnp.where(jnp.arange(w_in.shape[1] - n_qkv) >= 2 * d, 0.5, 1.0).astype(F32)
    xr, yr, ga, gl = _inproj_wide(x, shift1, gain1, (w_in[:, n_qkv:] * col_scale).astype(BF16))

    outs, lses = [], []
    for (window, dilation), q, k, v in zip(ATTN_GROUPS, (q0, q1, q2), (k0, k1, k2), (v0, v1, v2)):
        o, lse = _attention_group(q, k, v, window, dilation)
        outs.append(o)
        lses.append(lse)

    h_fwd = _lru_pass(xr, conv_w, conv_b, lru_w_a[0], lru_b_a[0], lru_w_x[0], lru_b_x[0], lru_lambda[0],
                      reverse=False)
    lru = _lru_pass(xr, conv_w, conv_b, lru_w_a[1], lru_b_a[1], lru_w_x[1], lru_b_x[1], lru_lambda[1],
                    reverse=True, h_fwd=h_fwd, yr=yr)

    x1, h2, aff = _merge(outs, lses, lru, ga, gl, x, gate1, w_attn_branch, w_lru_branch, w_out, shift2,
                         norm2_w * (1.0 + scale2), w_router)

    capacity = max(1, CAPACITY_FACTOR * s // N_EXPERTS)
    n_chunks = s // min(ROUTE_CHUNK, s)
    slot, base_table = _route(aff, capacity)
    base_flat = base_table[:, :, :n_chunks + 1].reshape(-1)
    xe, gates = _dispatch(h2, slot, aff, base_flat, capacity)
    ys = _experts(xe, gates, w_gate, w_up, w_down)
    return _combine(ys, slot, base_flat, x1, gate2, capacity)


def kernel(x, c, positions, w_ada, b_ada, norm1_w, w_in, q_norm_w, k_norm_w, conv_w, conv_b, lru_w_a, lru_b_a,
           lru_w_x, lru_b_x, lru_lambda, w_attn_branch, w_lru_branch, w_out, norm2_w, w_router, w_gate, w_up,
           w_down):
    for l in range(w_ada.shape[0]):
        x = _layer(x, c, positions, w_ada[l], b_ada[l], norm1_w[l], w_in[l], q_norm_w[l], k_norm_w[l],
                   conv_w[l, :, 0, :], conv_b[l], lru_w_a[l], lru_b_a[l], lru_w_x[l], lru_b_x[l], lru_lambda[l],
                   w_attn_branch[l], w_lru_branch[l], w_out[l], norm2_w[l], w_router[l], w_gate[l], w_up[l],
                   w_down[l])
    return x
```
